```python
import jax, jax.numpy as jnp
from jax import lax
import numpy as np

D_MODEL = 4096
BATCH = 4
SEQ = 4096
DEPTH = 1

N_Q_HEADS = 64
N_KV_HEADS = 8
HEAD_DIM = 64
Q_PER_KV = N_Q_HEADS // N_KV_HEADS
ATTN_WIDTH = N_Q_HEADS * HEAD_DIM
KV_WIDTH = N_KV_HEADS * HEAD_DIM
WINDOW = 128
BLOCK = 128
ROPE_THETA = 500000.0
ROPE_DIM = HEAD_DIM // 4

GMLP_WIDTH = D_MODEL
GMLP_GROUPS = 8
GMLP_GROUP_DIM = GMLP_WIDTH // GMLP_GROUPS
GMLP_CHUNK = 128

NORM_EPS = 1e-5
LN_EPS = 1e-5

PROJ_SIZES = (ATTN_WIDTH, KV_WIDTH, KV_WIDTH, ATTN_WIDTH,
              GMLP_WIDTH, GMLP_WIDTH, GMLP_WIDTH, D_MODEL, D_MODEL)
PROJ_WIDTH = int(sum(PROJ_SIZES))
SPLIT_POINTS = tuple(int(s) for s in np.cumsum(PROJ_SIZES)[:-1])

kernel_name = "hybrid_swa_sink_gmlp_gated_merge"


def rms_norm(x, g):
    x32 = x.astype(jnp.float32)
    y = x32 * lax.rsqrt(jnp.mean(x32 * x32, axis=-1, keepdims=True) + NORM_EPS)
    return (y * g.astype(jnp.float32)).astype(x.dtype)


def layer_norm(x, g, b):
    x32 = x.astype(jnp.float32)
    mu = jnp.mean(x32, axis=-1, keepdims=True)
    xc = x32 - mu
    y = xc * lax.rsqrt(jnp.mean(xc * xc, axis=-1, keepdims=True) + LN_EPS)
    return (y * g.astype(jnp.float32) + b.astype(jnp.float32)).astype(x.dtype)


def rope_tables(positions, dtype):
    half = ROPE_DIM // 2
    inv_freq = ROPE_THETA ** (-jnp.arange(half, dtype=jnp.float32) * 2.0 / ROPE_DIM)
    ang = positions.astype(jnp.float32)[..., None] * inv_freq
    return jnp.cos(ang)[:, :, None, :].astype(dtype), jnp.sin(ang)[:, :, None, :].astype(dtype)


def partial_rope(t, cos, sin):
    half = ROPE_DIM // 2
    t1 = t[..., :half]
    t2 = t[..., half:ROPE_DIM]
    return jnp.concatenate([t1 * cos - t2 * sin, t2 * cos + t1 * sin, t[..., ROPE_DIM:]], axis=-1)


def sliding_window_sink_attention(q, k, v, sink):
    B, S = q.shape[0], q.shape[1]
    nb = S // BLOCK
    qb = q.reshape(B, nb, BLOCK, N_KV_HEADS, Q_PER_KV, HEAD_DIM)
    kb = k.reshape(B, nb, BLOCK, N_KV_HEADS, HEAD_DIM)
    vb = v.reshape(B, nb, BLOCK, N_KV_HEADS, HEAD_DIM)

    def with_prev(t):
        prev = jnp.concatenate([jnp.zeros_like(t[:, :1]), t[:, :-1]], axis=1)
        return jnp.concatenate([prev, t], axis=2)

    kband, vband = with_prev(kb), with_prev(vb)
    sink_g = sink.astype(jnp.float32).reshape(1, N_KV_HEADS, Q_PER_KV, 1, 1)
    qi = jnp.arange(BLOCK)[:, None]
    si = jnp.arange(2 * BLOCK)[None, :]
    band = (si <= qi + BLOCK) & (si > qi + BLOCK - WINDOW)
    scale = HEAD_DIM ** -0.5

    def one_block(args):
        idx, qx, kx, vx = args
        s = jnp.einsum('bqhgd,bshd->bhgqs', qx, kx,
                       preferred_element_type=jnp.float32) * scale
        mask = band & ((idx > 0) | (si >= BLOCK))
        s = jnp.where(mask, s, -jnp.inf)
        m = jnp.maximum(jnp.max(s, axis=-1, keepdims=True), sink_g)
        p = jnp.exp(s - m)
        denom = jnp.sum(p, axis=-1, keepdims=True) + jnp.exp(sink_g - m)
        return jnp.einsum('bhgqs,bshd->bqhgd', (p / denom).astype(vx.dtype), vx)

    xs = (jnp.arange(nb), jnp.moveaxis(qb, 1, 0), jnp.moveaxis(kband, 1, 0), jnp.moveaxis(vband, 1, 0))
    out = lax.map(one_block, xs)
    return jnp.moveaxis(out, 0, 1).reshape(B, S, ATTN_WIDTH)


def chunked_spatial_gating(u, v, w_s, b_s, ln_g, ln_b):
    B, S, W = v.shape
    nc = S // GMLP_CHUNK
    vn = layer_norm(v, ln_g, ln_b)
    vc = vn.reshape(B, nc, GMLP_CHUNK, GMLP_GROUPS, GMLP_GROUP_DIM)
    causal = jnp.tril(jnp.ones((GMLP_CHUNK, GMLP_CHUNK), dtype=bool))
    w = jnp.where(causal[None], w_s, jnp.zeros_like(w_s)).astype(v.dtype)
    mixed = jnp.einsum('gts,bnsgc->bntgc', w, vc) + b_s.T[:, :, None].astype(v.dtype)
    return u * mixed.reshape(B, S, W)


def setup_inputs(seed: int = 0) -> dict:
    key = jax.random.key(seed)
    ks = jax.random.split(key, 16)
    f32 = jnp.float32
    x = jax.random.normal(ks[0], (BATCH, SEQ, D_MODEL), f32)
    offsets = jax.random.randint(ks[1], (BATCH, 1), 0, 4096, dtype=jnp.int32)
    positions = offsets + jnp.arange(SEQ, dtype=jnp.int32)[None, :]
    norm_g = 1.0 + 0.02 * jax.random.normal(ks[2], (DEPTH, D_MODEL), f32)
    w_in = jax.random.normal(ks[3], (DEPTH, D_MODEL, PROJ_WIDTH), f32) * D_MODEL ** -0.5
    attn_sink = 0.5 * jax.random.normal(ks[4], (DEPTH, N_Q_HEADS), f32)
    gmlp_ln_g = 1.0 + 0.02 * jax.random.normal(ks[5], (DEPTH, GMLP_WIDTH), f32)
    gmlp_ln_b = 0.02 * jax.random.normal(ks[6], (DEPTH, GMLP_WIDTH), f32)
    w_spatial = jax.random.normal(ks[7], (DEPTH, GMLP_GROUPS, GMLP_CHUNK, GMLP_CHUNK), f32) * GMLP_CHUNK ** -0.5
    b_spatial = 1.0 + 0.1 * jax.random.normal(ks[8], (DEPTH, GMLP_GROUPS, GMLP_CHUNK), f32)
    w_up_attn = jax.random.normal(ks[9], (DEPTH, ATTN_WIDTH, D_MODEL), f32) * ATTN_WIDTH ** -0.5
    w_up_gmlp = jax.random.normal(ks[10], (DEPTH, GMLP_WIDTH, D_MODEL), f32) * GMLP_WIDTH ** -0.5
    w_out = jax.random.normal(ks[11], (DEPTH, D_MODEL, D_MODEL), f32) * D_MODEL ** -0.5
    final_norm_g = 1.0 + 0.02 * jax.random.normal(ks[12], (D_MODEL,), f32)
    return {"x": x, "positions": positions, "norm_g": norm_g, "w_in": w_in,
            "attn_sink": attn_sink, "gmlp_ln_g": gmlp_ln_g, "gmlp_ln_b": gmlp_ln_b,
            "w_spatial": w_spatial, "b_spatial": b_spatial, "w_up_attn": w_up_attn,
            "w_up_gmlp": w_up_gmlp, "w_out": w_out, "final_norm_g": final_norm_g}


def reference(x, positions, norm_g, w_in, attn_sink, gmlp_ln_g, gmlp_ln_b, w_spatial,
              b_spatial, w_up_attn, w_up_gmlp, w_out, final_norm_g):
    B, S = x.shape[0], x.shape[1]
    cos, sin = rope_tables(positions, x.dtype)
    for l in range(DEPTH):
        h = rms_norm(x, norm_g[l])
        proj = jnp.einsum('bsd,dp->bsp', h, w_in[l])
        q, k, v, gate_a, u, vg, gate_b, mg_a, mg_b = jnp.split(proj, SPLIT_POINTS, axis=-1)
        q = partial_rope(q.reshape(B, S, N_Q_HEADS, HEAD_DIM), cos, sin)
        k = partial_rope(k.reshape(B, S, N_KV_HEADS, HEAD_DIM), cos, sin)
        v = v.reshape(B, S, N_KV_HEADS, HEAD_DIM)
        attn = sliding_window_sink_attention(q, k, v, attn_sink[l])
        y_a = jnp.einsum('bsw,wd->bsd', attn * jax.nn.silu(gate_a), w_up_attn[l])
        sg = chunked_spatial_gating(jax.nn.gelu(u), jax.nn.gelu(vg), w_spatial[l], b_spatial[l],
                                    gmlp_ln_g[l], gmlp_ln_b[l])
        y_b = jnp.einsum('bsw,wd->bsd', sg * jax.nn.silu(gate_b), w_up_gmlp[l])
        merged = jax.nn.sigmoid(mg_a) * y_a + jax.nn.sigmoid(mg_b) * y_b
        x = x + jnp.einsum('bsd,de->bse', merged, w_out[l])
    return rms_norm(x, final_norm_g)
```

```python
import functools

import jax
import jax.numpy as jnp
from jax import lax
from jax.experimental import pallas as pl
from jax.experimental.pallas import tpu as pltpu

D_MODEL = 4096
SEQ = 4096
DEPTH = 1
N_Q_HEADS = 64
N_KV_HEADS = 8
HEAD_DIM = 64
Q_PER_KV = N_Q_HEADS // N_KV_HEADS
ATTN_WIDTH = N_Q_HEADS * HEAD_DIM
KV_WIDTH = N_KV_HEADS * HEAD_DIM
WINDOW = 128
BLOCK = 128
ROPE_THETA = 500000.0
ROPE_DIM = HEAD_DIM // 4
ROPE_HALF = ROPE_DIM // 2
GMLP_WIDTH = D_MODEL
GMLP_GROUPS = 8
GMLP_GROUP_DIM = GMLP_WIDTH // GMLP_GROUPS
GMLP_CHUNK = 128
NORM_EPS = 1e-5
LN_EPS = 1e-5

OFF_Q = 0
OFF_KV = ATTN_WIDTH
OFF_GATE_A = OFF_KV + 2 * KV_WIDTH
OFF_U = OFF_GATE_A + ATTN_WIDTH
OFF_GATE_B = OFF_U + 2 * GMLP_WIDTH
OFF_MERGE = OFF_GATE_B + GMLP_WIDTH

LANES = 128
VMEM_LIMIT = 56 * 1024 * 1024

TM = 1024
TN = 1024
TM_OUT = 512

BF16 = jnp.bfloat16
F32 = jnp.float32


def _params(*sem):
    return pltpu.CompilerParams(dimension_semantics=sem, vmem_limit_bytes=VMEM_LIMIT)


def _rope_table_kernel(pos_ref, invf_ref, cos_ref, sin_ref):
    ang = pos_ref[...].astype(F32) * invf_ref[...]
    cos_ref[...] = jnp.cos(ang)
    sin_ref[...] = jnp.sin(ang)


def _rope_tables(pos_col, invf_lanes):
    t = pos_col.shape[0]
    r = 2048
    return pl.pallas_call(
        _rope_table_kernel,
        grid=(t // r,),
        in_specs=[pl.BlockSpec((r, 1), lambda i: (i, 0)),
                  pl.BlockSpec((1, LANES), lambda i: (0, 0))],
        out_specs=[pl.BlockSpec((r, LANES), lambda i: (i, 0)),
                   pl.BlockSpec((r, LANES), lambda i: (i, 0))],
        out_shape=[jax.ShapeDtypeStruct((t, LANES), F32)] * 2,
        compiler_params=_params("arbitrary"),
        name="rope_tables",
    )(pos_col, invf_lanes)


def _rmsnorm_kernel(x_ref, g_ref, o_ref):
    x = x_ref[...]
    ms = jnp.mean(x * x, axis=-1, keepdims=True)
    y = x * lax.rsqrt(ms + NORM_EPS)
    o_ref[...] = (y * g_ref[...]).astype(o_ref.dtype)


def _rmsnorm(x2, g_row):
    t, d = x2.shape
    r = 256
    return pl.pallas_call(
        _rmsnorm_kernel,
        grid=(t // r,),
        in_specs=[pl.BlockSpec((r, d), lambda i: (i, 0)),
                  pl.BlockSpec((1, d), lambda i: (0, 0))],
        out_specs=pl.BlockSpec((r, d), lambda i: (i, 0)),
        out_shape=jax.ShapeDtypeStruct((t, d), BF16),
        compiler_params=_params("arbitrary"),
        name="pre_rmsnorm",
    )(x2, g_row)


def _rope_slab(a, cos, sin, lo, hi):
    up = pltpu.roll(a, LANES - ROPE_HALF, axis=1)
    dn = pltpu.roll(a, ROPE_HALF, axis=1)
    return jnp.where(lo, a * cos - up * sin, jnp.where(hi, a * cos + dn * sin, a))


def _rope_masks(rows):
    j = lax.broadcasted_iota(jnp.int32, (rows, LANES), 1) & (HEAD_DIM - 1)
    return j < ROPE_HALF, (j >= ROPE_HALF) & (j < ROPE_DIM)


def _proj_q_kernel(h_ref, w_ref, cos_ref, sin_ref, o_ref):
    acc = jnp.dot(h_ref[...], w_ref[...], preferred_element_type=F32)
    cos, sin = cos_ref[...], sin_ref[...]
    lo, hi = _rope_masks(acc.shape[0])
    scale = HEAD_DIM ** -0.5
    for s in range(acc.shape[1] // LANES):
        sl = slice(s * LANES, (s + 1) * LANES)
        o_ref[:, sl] = (_rope_slab(acc[:, sl], cos, sin, lo, hi) * scale).astype(o_ref.dtype)


def _proj_kv_kernel(h_ref, w_ref, cos_ref, sin_ref, o_ref):
    acc = jnp.dot(h_ref[...], w_ref[...], preferred_element_type=F32)
    cos, sin = cos_ref[...], sin_ref[...]
    lo, hi = _rope_masks(acc.shape[0])
    for s in range(acc.shape[1] // LANES):
        sl = slice(s * LANES, (s + 1) * LANES)
        a = acc[:, sl]
        if s * LANES < KV_WIDTH:
            a = _rope_slab(a, cos, sin, lo, hi)
        o_ref[:, sl] = a.astype(o_ref.dtype)


def _proj_act_kernel(h_ref, w_ref, o_ref, *, act):
    acc = jnp.dot(h_ref[...], w_ref[...], preferred_element_type=F32)
    for s in range(acc.shape[1] // LANES):
        sl = slice(s * LANES, (s + 1) * LANES)
        o_ref[:, sl] = act(acc[:, sl]).astype(o_ref.dtype)


def _in_proj(name, h, w, body, n_blocks, col_block_fn, extra=()):
    t, d = h.shape
    extra_specs = [pl.BlockSpec((TM, LANES), lambda n, m: (m, 0)) for _ in extra]
    return pl.pallas_call(
        body,
        grid=(n_blocks, t // TM),
        in_specs=[pl.BlockSpec((TM, d), lambda n, m: (m, 0)),
                  pl.BlockSpec((d, TN), lambda n, m: (0, col_block_fn(n)))] + extra_specs,
        out_specs=pl.BlockSpec((TM, TN), lambda n, m: (m, n)),
        out_shape=jax.ShapeDtypeStruct((t, n_blocks * TN), BF16),
        compiler_params=_params("arbitrary", "arbitrary"),
        name=name,
    )(h, w, *extra)


def _attn_kernel(sink_ref, q_ref, kvp_ref, kvc_ref, ga_ref, o_ref):
    blk = pl.program_id(0) % (SEQ // BLOCK)
    band_w = 2 * BLOCK
    qi = lax.broadcasted_iota(jnp.int32, (BLOCK, band_w), 0)
    si = lax.broadcasted_iota(jnp.int32, (BLOCK, band_w), 1)
    first_key = jnp.where(blk > 0, 0, BLOCK)
    mask = (si <= qi + BLOCK) & (si > qi + BLOCK - WINDOW) & (si >= first_key)
    lane = lax.broadcasted_iota(jnp.int32, (band_w, LANES), 1)
    left = lane < HEAD_DIM

    def split_heads(col):
        slab = jnp.concatenate([kvp_ref[:, col:col + LANES], kvc_ref[:, col:col + LANES]], axis=0).astype(F32)
        swapped = pltpu.roll(slab, HEAD_DIM, axis=1)
        zero = jnp.zeros_like(slab)
        even = (jnp.where(left, slab, zero).astype(BF16), jnp.where(left, zero, swapped).astype(BF16))
        odd = (jnp.where(left, swapped, zero).astype(BF16), jnp.where(left, zero, slab).astype(BF16))
        return even, odd

    def softmax(s, sink):
        s = jnp.where(mask, s, -jnp.inf)
        m = jnp.maximum(jnp.max(s, axis=-1, keepdims=True), sink)
        p = jnp.exp(s - m)
        denom = jnp.sum(p, axis=-1, keepdims=True) + jnp.exp(sink - m)
        return p / denom

    nt = (((1,), (1,)), ((), ()))
    for j in range(N_KV_HEADS // 2):
        k_heads = split_heads(j * LANES)
        v_heads = split_heads(KV_WIDTH + j * LANES)
        for e in range(2):
            kv_head = 2 * j + e
            k_l, k_r = k_heads[e]
            v_pair = jnp.concatenate(v_heads[e], axis=0)
            for pp in range(Q_PER_KV // 2):
                pair = kv_head * (Q_PER_KV // 2) + pp
                sl = slice(pair * LANES, (pair + 1) * LANES)
                q2 = q_ref[:, sl]
                s_l = lax.dot_general(q2, k_l, nt, preferred_element_type=F32)
                s_r = lax.dot_general(q2, k_r, nt, preferred_element_type=F32)
                p_l = softmax(s_l, sink_ref[2 * pair])
                p_r = softmax(s_r, sink_ref[2 * pair + 1])
                p2 = jnp.concatenate([p_l, p_r], axis=1).astype(BF16)
                o = jnp.dot(p2, v_pair, preferred_element_type=F32)
                o_ref[:, sl] = (o * ga_ref[:, sl].astype(F32)).astype(o_ref.dtype)


def _attention(q, kv, act_a, sink):
    t = q.shape[0]
    nb = t // BLOCK
    return pl.pallas_call(
        _attn_kernel,
        grid_spec=pltpu.PrefetchScalarGridSpec(
            num_scalar_prefetch=1,
            grid=(nb,),
            in_specs=[pl.BlockSpec((BLOCK, ATTN_WIDTH), lambda i, s: (i, 0)),
                      pl.BlockSpec((BLOCK, 2 * KV_WIDTH), lambda i, s: (jnp.maximum(i - 1, 0), 0)),
                      pl.BlockSpec((BLOCK, 2 * KV_WIDTH), lambda i, s: (i, 0)),
                      pl.BlockSpec((BLOCK, ATTN_WIDTH), lambda i, s: (i, 0))],
            out_specs=pl.BlockSpec((BLOCK, ATTN_WIDTH), lambda i, s: (i, 0)),
        ),
        out_shape=jax.ShapeDtypeStruct((t, ATTN_WIDTH), BF16),
        compiler_params=_params("arbitrary"),
        name="swa_sink_attention",
    )(sink, q, kv, kv, act_a)


def _up_attn_kernel(a_ref, w_ref, sa_ref, o_ref):
    acc = jnp.dot(a_ref[...], w_ref[...], preferred_element_type=F32)
    for s in range(acc.shape[1] // LANES):
        sl = slice(s * LANES, (s + 1) * LANES)
        o_ref[:, sl] = (sa_ref[:, sl].astype(F32) * acc[:, sl]).astype(o_ref.dtype)


def _up_gmlp_kernel(a_ref, w_ref, sb_ref, ya_ref, o_ref):
    acc = jnp.dot(a_ref[...], w_ref[...], preferred_element_type=F32)
    for s in range(acc.shape[1] // LANES):
        sl = slice(s * LANES, (s + 1) * LANES)
        merged = ya_ref[:, sl].astype(F32) + sb_ref[:, sl].astype(F32) * acc[:, sl]
        o_ref[:, sl] = merged.astype(o_ref.dtype)


def _up_proj(body, a, w, tiles, name):
    t, k = a.shape
    n = w.shape[1]
    tile_specs = [pl.BlockSpec((TM, TN), functools.partial(lambda j, i, off: (i, j + off), off=off))
                  for _, off in tiles]
    return pl.pallas_call(
        body,
        grid=(n // TN, t // TM),
        in_specs=[pl.BlockSpec((TM, k), lambda j, i: (i, 0)),
                  pl.BlockSpec((k, TN), lambda j, i: (0, j))] + tile_specs,
        out_specs=pl.BlockSpec((TM, TN), lambda j, i: (i, j)),
        out_shape=jax.ShapeDtypeStruct((t, n), BF16),
        compiler_params=_params("arbitrary", "arbitrary"),
        name=name,
    )(a, w, *[arr for arr, _ in tiles])


def _gating_kernel(u_ref, v_ref, gb_ref, lng_ref, lnb_ref, ws_ref, bt_ref, o_ref):
    v = v_ref[...].astype(F32)
    mu = jnp.mean(v, axis=-1, keepdims=True)
    vc = v - mu
    var = jnp.mean(vc * vc, axis=-1, keepdims=True)
    vn = (vc * lax.rsqrt(var + LN_EPS) * lng_ref[...] + lnb_ref[...]).astype(BF16)
    ti = lax.broadcasted_iota(jnp.int32, (GMLP_CHUNK, GMLP_CHUNK), 0)
    si = lax.broadcasted_iota(jnp.int32, (GMLP_CHUNK, GMLP_CHUNK), 1)
    causal = si <= ti
    bt = bt_ref[...]
    for g in range(GMLP_GROUPS):
        w = jnp.where(causal, ws_ref[g], 0.0).astype(BF16)
        bias = bt[:, g:g + 1]
        cols = slice(g * GMLP_GROUP_DIM, (g + 1) * GMLP_GROUP_DIM)
        for c in range(v.shape[0] // GMLP_CHUNK):
            rows = slice(c * GMLP_CHUNK, (c + 1) * GMLP_CHUNK)
            mixed = jnp.dot(w, vn[rows, cols], preferred_element_type=F32) + bias
            sg = u_ref[rows, cols].astype(F32) * mixed
            o_ref[rows, cols] = (sg * gb_ref[rows, cols].astype(F32)).astype(o_ref.dtype)


def _spatial_gating(act_uv, act_gate, ln_g, ln_b, w_s, b_t):
    t = act_uv.shape[0]
    r = 2 * GMLP_CHUNK
    w = GMLP_WIDTH
    return pl.pallas_call(
        _gating_kernel,
        grid=(t // r,),
        in_specs=[pl.BlockSpec((r, w), lambda i: (i, 0)),
                  pl.BlockSpec((r, w), lambda i: (i, 1)),
                  pl.BlockSpec((r, w), lambda i: (i, 1)),
                  pl.BlockSpec((1, w), lambda i: (0, 0)),
                  pl.BlockSpec((1, w), lambda i: (0, 0)),
                  pl.BlockSpec((GMLP_GROUPS, GMLP_CHUNK, GMLP_CHUNK), lambda i: (0, 0, 0)),
                  pl.BlockSpec((GMLP_CHUNK, GMLP_GROUPS), lambda i: (0, 0))],
        out_specs=pl.BlockSpec((r, w), lambda i: (i, 0)),
        out_shape=jax.ShapeDtypeStruct((t, w), BF16),
        compiler_params=_params("arbitrary"),
        name="spatial_gating",
    )(act_uv, act_uv, act_gate, ln_g, ln_b, w_s, b_t)


def _out_kernel(a_ref, w_ref, x_ref, g_ref, o_ref):
    j = pl.program_id(1)
    acc = jnp.dot(a_ref[...], w_ref[...], preferred_element_type=F32)
    col = pl.multiple_of(j * TN, TN)
    o_ref[:, pl.ds(col, TN)] = x_ref[...] + acc

    @pl.when(j == pl.num_programs(1) - 1)
    def _():
        y = o_ref[...]
        ms = jnp.mean(y * y, axis=-1, keepdims=True)
        o_ref[...] = y * lax.rsqrt(ms + NORM_EPS) * g_ref[...]


def _out_proj(a, w, x2, g_row):
    t, k = a.shape
    n = w.shape[1]
    return pl.pallas_call(
        _out_kernel,
        grid=(t // TM_OUT, n // TN),
        in_specs=[pl.BlockSpec((TM_OUT, k), lambda i, j: (i, 0)),
                  pl.BlockSpec((k, TN), lambda i, j: (0, j)),
                  pl.BlockSpec((TM_OUT, TN), lambda i, j: (i, j)),
                  pl.BlockSpec((1, n), lambda i, j: (0, 0))],
        out_specs=pl.BlockSpec((TM_OUT, n), lambda i, j: (i, 0)),
        out_shape=jax.ShapeDtypeStruct((t, n), F32),
        compiler_params=_params("arbitrary", "arbitrary"),
        name="out_proj_residual_norm",
    )(a, w, x2, g_row)


def _rope_inv_freq_lanes():
    inv_freq = ROPE_THETA ** (-jnp.arange(ROPE_HALF, dtype=F32) * 2.0 / ROPE_DIM)
    j = jnp.arange(LANES) % HEAD_DIM
    return jnp.where(j < ROPE_DIM, inv_freq[j % ROPE_HALF], 0.0).astype(F32)[None, :]


def kernel(x, positions, norm_g, w_in, attn_sink, gmlp_ln_g, gmlp_ln_b, w_spatial, b_spatial, w_up_attn,
           w_up_gmlp, w_out, final_norm_g):
    b, s, d = x.shape
    t = b * s
    assert s == SEQ and d == D_MODEL and norm_g.shape[0] == DEPTH == 1
    x2 = x.reshape(t, d)
    cos_t, sin_t = _rope_tables(positions.reshape(t, 1), _rope_inv_freq_lanes())
    nq, nkv, nw = ATTN_WIDTH // TN, 2 * KV_WIDTH // TN, GMLP_WIDTH // TN
    w_in_b = w_in[0].astype(BF16)
    h = _rmsnorm(x2, norm_g)
    q = _in_proj("in_proj_q", h, w_in_b, _proj_q_kernel, nq, lambda n: n + OFF_Q // TN, (cos_t, sin_t))
    kv = _in_proj("in_proj_kv", h, w_in_b, _proj_kv_kernel, nkv, lambda n: n + OFF_KV // TN, (cos_t, sin_t))
    act_gate = _in_proj("in_proj_silu", h, w_in_b, functools.partial(_proj_act_kernel, act=jax.nn.silu), 2 * nw,
                        lambda n: jnp.where(n < nw, n + OFF_GATE_A // TN, n - nw + OFF_GATE_B // TN))
    act_uv = _in_proj("in_proj_gelu", h, w_in_b, functools.partial(_proj_act_kernel, act=jax.nn.gelu), 2 * nw,
                      lambda n: n + OFF_U // TN)
    act_merge = _in_proj("in_proj_sigmoid", h, w_in_b, functools.partial(_proj_act_kernel, act=jax.nn.sigmoid),
                         2 * nw, lambda n: n + OFF_MERGE // TN)
    attn_g = _attention(q, kv, act_gate, attn_sink[0])
    ya = _up_proj(_up_attn_kernel, attn_g, w_up_attn[0].astype(BF16), [(act_merge, 0)], "up_attn")
    sgb = _spatial_gating(act_uv, act_gate, gmlp_ln_g, gmlp_ln_b, w_spatial[0], b_spatial[0].T)
    merged = _up_proj(_up_gmlp_kernel, sgb, w_up_gmlp[0].astype(BF16), [(act_merge, nw), (ya, 0)],
                      "up_gmlp_merge")
    out = _out_proj(merged, w_out[0].astype(BF16), x2, final_norm_g[None, :])
    return out.reshape(b, s, d)
```

```python
import functools

import jax
import jax.numpy as jnp
from jax import lax
from jax.experimental import pallas as pl
from jax.experimental.pallas import tpu as pltpu

D_MODEL = 4096
SEQ = 4096
DEPTH = 1
N_Q_HEADS = 64
N_KV_HEADS = 8
HEAD_DIM = 64
Q_PER_KV = N_Q_HEADS // N_KV_HEADS
ATTN_WIDTH = N_Q_HEADS * HEAD_DIM
KV_WIDTH = N_KV_HEADS * HEAD_DIM
WINDOW = 128
BLOCK = 128
ROPE_THETA = 500000.0
ROPE_DIM = HEAD_DIM // 4
ROPE_HALF = ROPE_DIM // 2
GMLP_WIDTH = D_MODEL
GMLP_GROUPS = 8
GMLP_GROUP_DIM = GMLP_WIDTH // GMLP_GROUPS
GMLP_CHUNK = 128
NORM_EPS = 1e-5
LN_EPS = 1e-5

OFF_Q = 0
OFF_KV = ATTN_WIDTH
OFF_GATE_A = OFF_KV + 2 * KV_WIDTH
OFF_U = OFF_GATE_A + ATTN_WIDTH
OFF_GATE_B = OFF_U + 2 * GMLP_WIDTH
OFF_MERGE = OFF_GATE_B + GMLP_WIDTH

LANES = 128
VMEM_LIMIT = 56 * 1024 * 1024

TM = 1024
TN = 1024
TM_IN = 1024
TN_IN = 1024
TM_OUT = 512

BF16 = jnp.bfloat16
F32 = jnp.float32


def _params(*sem):
    return pltpu.CompilerParams(dimension_semantics=sem, vmem_limit_bytes=VMEM_LIMIT)


def _rope_table_kernel(pos_ref, invf_ref, cos_ref, sin_ref):
    ang = pos_ref[...].astype(F32) * invf_ref[...]
    cos_ref[...] = jnp.cos(ang)
    sin_ref[...] = jnp.sin(ang)


def _rope_tables(pos_col, invf_lanes):
    t = pos_col.shape[0]
    r = 2048
    return pl.pallas_call(
        _rope_table_kernel,
        grid=(t // r,),
        in_specs=[pl.BlockSpec((r, 1), lambda i: (i, 0)),
                  pl.BlockSpec((1, LANES), lambda i: (0, 0))],
        out_specs=[pl.BlockSpec((r, LANES), lambda i: (i, 0)),
                   pl.BlockSpec((r, LANES), lambda i: (i, 0))],
        out_shape=[jax.ShapeDtypeStruct((t, LANES), F32)] * 2,
        compiler_params=_params("arbitrary"),
        name="rope_tables",
    )(pos_col, invf_lanes)


def _rmsnorm_kernel(x_ref, g_ref, o_ref):
    x = x_ref[...]
    ms = jnp.mean(x * x, axis=-1, keepdims=True)
    y = x * lax.rsqrt(ms + NORM_EPS)
    o_ref[...] = (y * g_ref[...]).astype(o_ref.dtype)


def _rmsnorm(x2, g_row):
    t, d = x2.shape
    r = 256
    return pl.pallas_call(
        _rmsnorm_kernel,
        grid=(t // r,),
        in_specs=[pl.BlockSpec((r, d), lambda i: (i, 0)),
                  pl.BlockSpec((1, d), lambda i: (0, 0))],
        out_specs=pl.BlockSpec((r, d), lambda i: (i, 0)),
        out_shape=jax.ShapeDtypeStruct((t, d), BF16),
        compiler_params=_params("arbitrary"),
        name="pre_rmsnorm",
    )(x2, g_row)


def _rope_slab(a, cos, sin, lo, hi):
    up = pltpu.roll(a, LANES - ROPE_HALF, axis=1)
    dn = pltpu.roll(a, ROPE_HALF, axis=1)
    return jnp.where(lo, a * cos - up * sin, jnp.where(hi, a * cos + dn * sin, a))


def _rope_masks(rows):
    j = lax.broadcasted_iota(jnp.int32, (rows, LANES), 1) & (HEAD_DIM - 1)
    return j < ROPE_HALF, (j >= ROPE_HALF) & (j < ROPE_DIM)


def _proj_q_kernel(h_ref, w_ref, cos_ref, sin_ref, o_ref):
    acc = jnp.dot(h_ref[...], w_ref[...], preferred_element_type=F32)
    cos, sin = cos_ref[...], sin_ref[...]
    lo, hi = _rope_masks(acc.shape[0])
    scale = HEAD_DIM ** -0.5
    for s in range(acc.shape[1] // LANES):
        sl = slice(s * LANES, (s + 1) * LANES)
        o_ref[:, sl] = (_rope_slab(acc[:, sl], cos, sin, lo, hi) * scale).astype(o_ref.dtype)


def _proj_kv_kernel(h_ref, w_ref, cos_ref, sin_ref, o_ref):
    acc = jnp.dot(h_ref[...], w_ref[...], preferred_element_type=F32)
    cos, sin = cos_ref[...], sin_ref[...]
    lo, hi = _rope_masks(acc.shape[0])
    for s in range(acc.shape[1] // LANES):
        sl = slice(s * LANES, (s + 1) * LANES)
        a = acc[:, sl]
        if s * LANES < KV_WIDTH:
            a = _rope_slab(a, cos, sin, lo, hi)
        o_ref[:, sl] = a.astype(o_ref.dtype)


def _sigmoid(x):
    return 0.5 * jnp.tanh(0.5 * x) + 0.5


def _silu(x):
    half = 0.5 * x
    return half * jnp.tanh(half) + half


def _proj_act_kernel(h_ref, w_ref, o_ref, *, act):
    acc = jnp.dot(h_ref[...], w_ref[...], preferred_element_type=F32)
    for s in range(acc.shape[1] // LANES):
        sl = slice(s * LANES, (s + 1) * LANES)
        o_ref[:, sl] = act(acc[:, sl]).astype(o_ref.dtype)


def _in_proj(name, h, w, body, col_ranges, width, tm, tn, extra=()):
    t, d = h.shape
    per_range = width // tn
    n_blocks = per_range * len(col_ranges)
    starts = [c // tn for c in col_ranges]

    def w_block(n):
        blk = n % per_range + starts[0]
        for r in range(1, len(starts)):
            blk = jnp.where(n // per_range == r, n % per_range + starts[r], blk)
        return blk

    extra_specs = [pl.BlockSpec((tm, LANES), lambda n, m: (m, 0)) for _ in extra]
    return pl.pallas_call(
        body,
        grid=(n_blocks, t // tm),
        in_specs=[pl.BlockSpec((tm, d), lambda n, m: (m, 0)),
                  pl.BlockSpec((d, tn), lambda n, m: (0, w_block(n)))] + extra_specs,
        out_specs=pl.BlockSpec((tm, tn), lambda n, m: (m, n)),
        out_shape=jax.ShapeDtypeStruct((t, n_blocks * tn), BF16),
        compiler_params=_params("arbitrary", "arbitrary"),
        name=name,
    )(h, w, *extra)


def _attn_kernel(sink_ref, q_ref, kvp_ref, kvc_ref, ga_ref, o_ref, bias_ref):
    blk = pl.program_id(0) % (SEQ // BLOCK)
    band = 2 * BLOCK
    pairs = Q_PER_KV // 2
    si = lax.broadcasted_iota(jnp.int32, (band, LANES), 0)
    qi = lax.broadcasted_iota(jnp.int32, (band, LANES), 1)
    first_key = jnp.where(blk > 0, 0, BLOCK)
    visible = (si <= qi + BLOCK) & (si > qi + BLOCK - WINDOW) & (si >= first_key)
    bias = jnp.where(visible, 0.0, -jnp.inf).astype(F32)
    bias_ref[...] = jnp.concatenate([bias] * pairs, axis=1)
    left = qi < HEAD_DIM
    lane_pair = lax.broadcasted_iota(jnp.int32, (1, pairs * LANES), 1) // LANES
    zeros_t = jnp.zeros((HEAD_DIM, band), F32)
    nt = (((1,), (1,)), ((), ()))

    def softmax_t(s, sink):
        m = jnp.maximum(jnp.max(s, axis=0, keepdims=True), sink)
        p = jnp.exp(s - m)
        denom = jnp.sum(p, axis=0, keepdims=True) + jnp.exp(sink - m)
        return p.astype(BF16), 1.0 / denom

    for j in range(N_KV_HEADS // 2):
        kcol, vcol = j * LANES, KV_WIDTH + j * LANES
        kslab = jnp.concatenate([kvp_ref[:, kcol:kcol + LANES], kvc_ref[:, kcol:kcol + LANES]], axis=0).astype(F32)
        kswap = pltpu.roll(kslab, HEAD_DIM, axis=1)
        zero = jnp.zeros_like(kslab)
        vslab = jnp.concatenate([kvp_ref[:, vcol:vcol + LANES], kvc_ref[:, vcol:vcol + LANES]], axis=0).astype(F32)
        v_t = vslab.T
        for e in range(2):
            kv_head = 2 * j + e
            if e == 0:
                k_l, k_r = jnp.where(left, kslab, zero), jnp.where(left, zero, kswap)
            else:
                k_l, k_r = jnp.where(left, kswap, zero), jnp.where(left, zero, kslab)
            vt = v_t[e * HEAD_DIM:(e + 1) * HEAD_DIM]
            v_bd = jnp.concatenate([jnp.concatenate([vt, zeros_t], axis=1),
                                    jnp.concatenate([zeros_t, vt], axis=1)], axis=0).astype(BF16)
            slabs = [slice((kv_head * pairs + pp) * LANES, (kv_head * pairs + pp + 1) * LANES) for pp in range(pairs)]
            q_rows = jnp.concatenate([q_ref[:, sl] for sl in slabs], axis=0)
            sink_l = jnp.zeros((1, pairs * LANES), F32)
            sink_r = jnp.zeros((1, pairs * LANES), F32)
            for pp in range(pairs):
                head = 2 * (kv_head * pairs + pp)
                sink_l = jnp.where(lane_pair == pp, sink_ref[head], sink_l)
                sink_r = jnp.where(lane_pair == pp, sink_ref[head + 1], sink_r)
            s_l = lax.dot_general(k_l.astype(BF16), q_rows, nt, preferred_element_type=F32) + bias_ref[...]
            s_r = lax.dot_general(k_r.astype(BF16), q_rows, nt, preferred_element_type=F32) + bias_ref[...]
            p_l, r_l = softmax_t(s_l, sink_l)
            p_r, r_r = softmax_t(s_r, sink_r)
            p2 = jnp.concatenate([p_l, p_r], axis=0)
            o_t = jnp.dot(v_bd, p2, preferred_element_type=F32)
            o_t = jnp.concatenate([o_t[:HEAD_DIM] * r_l, o_t[HEAD_DIM:] * r_r], axis=0)
            for pp, sl in enumerate(slabs):
                o = o_t[:, pp * LANES:(pp + 1) * LANES].T
                o_ref[:, sl] = (o * ga_ref[:, sl].astype(F32)).astype(o_ref.dtype)


def _attention(q, kv, act_a, sink):
    t = q.shape[0]
    nb = t // BLOCK
    return pl.pallas_call(
        _attn_kernel,
        grid_spec=pltpu.PrefetchScalarGridSpec(
            num_scalar_prefetch=1,
            grid=(nb,),
            in_specs=[pl.BlockSpec((BLOCK, ATTN_WIDTH), lambda i, s: (i, 0)),
                      pl.BlockSpec((BLOCK, 2 * KV_WIDTH), lambda i, s: (jnp.maximum(i - 1, 0), 0)),
                      pl.BlockSpec((BLOCK, 2 * KV_WIDTH), lambda i, s: (i, 0)),
                      pl.BlockSpec((BLOCK, ATTN_WIDTH), lambda i, s: (i, 0))],
            out_specs=pl.BlockSpec((BLOCK, ATTN_WIDTH), lambda i, s: (i, 0)),
            scratch_shapes=[pltpu.VMEM((2 * BLOCK, (Q_PER_KV // 2) * LANES), F32)],
        ),
        out_shape=jax.ShapeDtypeStruct((t, ATTN_WIDTH), BF16),
        compiler_params=_params("arbitrary"),
        name="swa_sink_attention",
    )(sink, q, kv, kv, act_a)


def _up_attn_kernel(a_ref, w_ref, sa_ref, o_ref):
    acc = jnp.dot(a_ref[...], w_ref[...], preferred_element_type=F32)
    for s in range(acc.shape[1] // LANES):
        sl = slice(s * LANES, (s + 1) * LANES)
        o_ref[:, sl] = (sa_ref[:, sl].astype(F32) * acc[:, sl]).astype(o_ref.dtype)


def _up_gmlp_kernel(a_ref, w_ref, sb_ref, ya_ref, o_ref):
    acc = jnp.dot(a_ref[...], w_ref[...], preferred_element_type=F32)
    for s in range(acc.shape[1] // LANES):
        sl = slice(s * LANES, (s + 1) * LANES)
        merged = ya_ref[:, sl].astype(F32) + sb_ref[:, sl].astype(F32) * acc[:, sl]
        o_ref[:, sl] = merged.astype(o_ref.dtype)


def _up_proj(body, a, w, tiles, name):
    t, k = a.shape
    n = w.shape[1]
    tile_specs = [pl.BlockSpec((TM, TN), functools.partial(lambda j, i, off: (i, j + off), off=off))
                  for _, off in tiles]
    return pl.pallas_call(
        body,
        grid=(n // TN, t // TM),
        in_specs=[pl.BlockSpec((TM, k), lambda j, i: (i, 0)),
                  pl.BlockSpec((k, TN), lambda j, i: (0, j))] + tile_specs,
        out_specs=pl.BlockSpec((TM, TN), lambda j, i: (i, j)),
        out_shape=jax.ShapeDtypeStruct((t, n), BF16),
        compiler_params=_params("arbitrary", "arbitrary"),
        name=name,
    )(a, w, *[arr for arr, _ in tiles])


def _gating_kernel(u_ref, v_ref, gb_ref, lng_ref, lnb_ref, ws_ref, bt_ref, o_ref):
    v = v_ref[...].astype(F32)
    mu = jnp.mean(v, axis=-1, keepdims=True)
    vc = v - mu
    var = jnp.mean(vc * vc, axis=-1, keepdims=True)
    vn = (vc * lax.rsqrt(var + LN_EPS) * lng_ref[...] + lnb_ref[...]).astype(BF16)
    ti = lax.broadcasted_iota(jnp.int32, (GMLP_CHUNK, GMLP_CHUNK), 0)
    si = lax.broadcasted_iota(jnp.int32, (GMLP_CHUNK, GMLP_CHUNK), 1)
    causal = si <= ti
    bt = bt_ref[...]
    for g in range(GMLP_GROUPS):
        w = jnp.where(causal, ws_ref[g], 0.0).astype(BF16)
        bias = bt[:, g:g + 1]
        cols = slice(g * GMLP_GROUP_DIM, (g + 1) * GMLP_GROUP_DIM)
        for c in range(v.shape[0] // GMLP_CHUNK):
            rows = slice(c * GMLP_CHUNK, (c + 1) * GMLP_CHUNK)
            mixed = jnp.dot(w, vn[rows, cols], preferred_element_type=F32) + bias
            sg = u_ref[rows, cols].astype(F32) * mixed
            o_ref[rows, cols] = (sg * gb_ref[rows, cols].astype(F32)).astype(o_ref.dtype)


def _spatial_gating(act_uv, act_gate, ln_g, ln_b, w_s, b_t):
    t = act_uv.shape[0]
    r = 2 * GMLP_CHUNK
    w = GMLP_WIDTH
    return pl.pallas_call(
        _gating_kernel,
        grid=(t // r,),
        in_specs=[pl.BlockSpec((r, w), lambda i: (i, 0)),
                  pl.BlockSpec((r, w), lambda i: (i, 1)),
                  pl.BlockSpec((r, w), lambda i: (i, 1)),
                  pl.BlockSpec((1, w), lambda i: (0, 0)),
                  pl.BlockSpec((1, w), lambda i: (0, 0)),
                  pl.BlockSpec((GMLP_GROUPS, GMLP_CHUNK, GMLP_CHUNK), lambda i: (0, 0, 0)),
                  pl.BlockSpec((GMLP_CHUNK, GMLP_GROUPS), lambda i: (0, 0))],
        out_specs=pl.BlockSpec((r, w), lambda i: (i, 0)),
        out_shape=jax.ShapeDtypeStruct((t, w), BF16),
        compiler_params=_params("arbitrary"),
        name="spatial_gating",
    )(act_uv, act_uv, act_gate, ln_g, ln_b, w_s, b_t)


def _out_kernel(a_ref, w_ref, x_ref, g_ref, o_ref, y0_ref, y1_ref, ss0_ref, ss1_ref):
    i, j = pl.program_id(0), pl.program_id(1)
    col = pl.multiple_of(j * TN, TN)

    @pl.when((i == 0) & (j == 0))
    def _():
        y1_ref[...] = jnp.zeros_like(y1_ref)
        ss1_ref[...] = jnp.zeros_like(ss1_ref)

    def step(y_ref, ss_ref, yp_ref, ssp_ref):
        ms_prev = jnp.sum(ssp_ref[...], axis=-1, keepdims=True) * (1.0 / D_MODEL)
        o_ref[...] = yp_ref[:, pl.ds(col, TN)] * lax.rsqrt(ms_prev + NORM_EPS) * g_ref[:, pl.ds(col, TN)]
        y = x_ref[...] + jnp.dot(a_ref[...], w_ref[...], preferred_element_type=F32)
        sq = y * y
        part = sq[:, :LANES]
        for s in range(1, TN // LANES):
            part = part + sq[:, s * LANES:(s + 1) * LANES]
        ss_ref[...] = jnp.where(j == 0, part, ss_ref[...] + part)
        y_ref[:, pl.ds(col, TN)] = y

    pl.when(i % 2 == 0)(lambda: step(y0_ref, ss0_ref, y1_ref, ss1_ref))
    pl.when(i % 2 == 1)(lambda: step(y1_ref, ss1_ref, y0_ref, ss0_ref))


def _out_proj(a, w, x2, g_row):
    t, k = a.shape
    n = w.shape[1]
    last = t // TM_OUT - 1
    return pl.pallas_call(
        _out_kernel,
        grid=(t // TM_OUT + 1, n // TN),
        in_specs=[pl.BlockSpec((TM_OUT, k), lambda i, j: (jnp.minimum(i, last), 0)),
                  pl.BlockSpec((k, TN), lambda i, j: (0, j)),
                  pl.BlockSpec((TM_OUT, TN), lambda i, j: (jnp.minimum(i, last), j)),
                  pl.BlockSpec((1, n), lambda i, j: (0, 0))],
        out_specs=pl.BlockSpec((TM_OUT, TN), lambda i, j: (jnp.maximum(i - 1, 0), jnp.where(i == 0, 0, j))),
        out_shape=jax.ShapeDtypeStruct((t, n), F32),
        scratch_shapes=[pltpu.VMEM((TM_OUT, n), F32), pltpu.VMEM((TM_OUT, n), F32),
                        pltpu.VMEM((TM_OUT, LANES), F32), pltpu.VMEM((TM_OUT, LANES), F32)],
        compiler_params=_params("arbitrary", "arbitrary"),
        name="out_proj_residual_norm",
    )(a, w, x2, g_row)


def _rope_inv_freq_lanes():
    inv_freq = ROPE_THETA ** (-jnp.arange(ROPE_HALF, dtype=F32) * 2.0 / ROPE_DIM)
    j = jnp.arange(LANES) % HEAD_DIM
    return jnp.where(j < ROPE_DIM, inv_freq[j % ROPE_HALF], 0.0).astype(F32)[None, :]


def kernel(x, positions, norm_g, w_in, attn_sink, gmlp_ln_g, gmlp_ln_b, w_spatial, b_spatial, w_up_attn,
           w_up_gmlp, w_out, final_norm_g):
    b, s, d = x.shape
    t = b * s
    assert s == SEQ and d == D_MODEL and norm_g.shape[0] == DEPTH == 1
    x2 = x.reshape(t, d)
    cos_t, sin_t = _rope_tables(positions.reshape(t, 1), _rope_inv_freq_lanes())
    nw = GMLP_WIDTH // TN
    w_in_b = w_in[0].astype(BF16)
    h = _rmsnorm(x2, norm_g)
    q = _in_proj("in_proj_q", h, w_in_b, _proj_q_kernel, [OFF_Q], ATTN_WIDTH, TM_IN, TN_IN, (cos_t, sin_t))
    kv = _in_proj("in_proj_kv", h, w_in_b, _proj_kv_kernel, [OFF_KV], 2 * KV_WIDTH, TM, TN, (cos_t, sin_t))
    act_gate = _in_proj("in_proj_silu", h, w_in_b, functools.partial(_proj_act_kernel, act=_silu),
                        [OFF_GATE_A, OFF_GATE_B], ATTN_WIDTH, TM_IN, TN_IN)
    act_uv = _in_proj("in_proj_gelu", h, w_in_b, functools.partial(_proj_act_kernel, act=jax.nn.gelu),
                      [OFF_U], 2 * GMLP_WIDTH, TM_IN, TN_IN)
    act_merge = _in_proj("in_proj_sigmoid", h, w_in_b, functools.partial(_proj_act_kernel, act=_sigmoid),
                         [OFF_MERGE], 2 * D_MODEL, TM_IN, TN_IN)
    attn_g = _attention(q, kv, act_gate, attn_sink[0])
    ya = _up_proj(_up_attn_kernel, attn_g, w_up_attn[0].astype(BF16), [(act_merge, 0)], "up_attn")
    sgb = _spatial_gating(act_uv, act_gate, gmlp_ln_g, gmlp_ln_b, w_spatial[0], b_spatial[0].T)
    merged = _up_proj(_up_gmlp_kernel, sgb, w_up_gmlp[0].astype(BF16), [(act_merge, nw), (ya, 0)],
                      "up_gmlp_merge")
    out = _out_proj(merged, w_out[0].astype(BF16), x2, final_norm_g[None, :])
    return out.reshape(b, s, d)
```

```python
import functools

import jax
import jax.numpy as jnp
from jax import lax
from jax.experimental import pallas as pl
from jax.experimental.pallas import tpu as pltpu

D_MODEL = 4096
SEQ = 4096
DEPTH = 1
N_Q_HEADS = 64
N_KV_HEADS = 8
HEAD_DIM = 64
Q_PER_KV = N_Q_HEADS // N_KV_HEADS
ATTN_WIDTH = N_Q_HEADS * HEAD_DIM
KV_WIDTH = N_KV_HEADS * HEAD_DIM
WINDOW = 128
BLOCK = 128
ROPE_THETA = 500000.0
ROPE_DIM = HEAD_DIM // 4
ROPE_HALF = ROPE_DIM // 2
GMLP_WIDTH = D_MODEL
GMLP_GROUPS = 8
GMLP_GROUP_DIM = GMLP_WIDTH // GMLP_GROUPS
GMLP_CHUNK = 128
NORM_EPS = 1e-5
LN_EPS = 1e-5

OFF_Q = 0
OFF_KV = ATTN_WIDTH
OFF_GATE_A = OFF_KV + 2 * KV_WIDTH
OFF_U = OFF_GATE_A + ATTN_WIDTH
OFF_GATE_B = OFF_U + 2 * GMLP_WIDTH
OFF_MERGE = OFF_GATE_B + GMLP_WIDTH

LANES = 128
VMEM_LIMIT = 56 * 1024 * 1024

TM = 1024
TN = 1024
TM_IN = 1024
TN_IN = 1024
TM_OUT = 512

BF16 = jnp.bfloat16
F32 = jnp.float32


def _params(*sem):
    return pltpu.CompilerParams(dimension_semantics=sem, vmem_limit_bytes=VMEM_LIMIT)


def _rope_table_kernel(pos_ref, invf_ref, cos_ref, sin_ref):
    ang = pos_ref[...].astype(F32) * invf_ref[...]
    cos_ref[...] = jnp.cos(ang)
    sin_ref[...] = jnp.sin(ang)


def _rope_tables(pos_col, invf_lanes):
    t = pos_col.shape[0]
    r = 2048
    return pl.pallas_call(
        _rope_table_kernel,
        grid=(t // r,),
        in_specs=[pl.BlockSpec((r, 1), lambda i: (i, 0)),
                  pl.BlockSpec((1, LANES), lambda i: (0, 0))],
        out_specs=[pl.BlockSpec((r, LANES), lambda i: (i, 0)),
                   pl.BlockSpec((r, LANES), lambda i: (i, 0))],
        out_shape=[jax.ShapeDtypeStruct((t, LANES), F32)] * 2,
        compiler_params=_params("arbitrary"),
        name="rope_tables",
    )(pos_col, invf_lanes)


def _rmsnorm_kernel(x_ref, g_ref, o_ref):
    x = x_ref[...]
    ms = jnp.mean(x * x, axis=-1, keepdims=True)
    y = x * lax.rsqrt(ms + NORM_EPS)
    o_ref[...] = (y * g_ref[...]).astype(o_ref.dtype)


def _rmsnorm(x2, g_row):
    t, d = x2.shape
    r = 256
    return pl.pallas_call(
        _rmsnorm_kernel,
        grid=(t // r,),
        in_specs=[pl.BlockSpec((r, d), lambda i: (i, 0)),
                  pl.BlockSpec((1, d), lambda i: (0, 0))],
        out_specs=pl.BlockSpec((r, d), lambda i: (i, 0)),
        out_shape=jax.ShapeDtypeStruct((t, d), BF16),
        compiler_params=_params("arbitrary"),
        name="pre_rmsnorm",
    )(x2, g_row)


def _rope_slab(a, cos, sin, lo, hi):
    up = pltpu.roll(a, LANES - ROPE_HALF, axis=1)
    dn = pltpu.roll(a, ROPE_HALF, axis=1)
    return jnp.where(lo, a * cos - up * sin, jnp.where(hi, a * cos + dn * sin, a))


def _rope_masks(rows):
    j = lax.broadcasted_iota(jnp.int32, (rows, LANES), 1) & (HEAD_DIM - 1)
    return j < ROPE_HALF, (j >= ROPE_HALF) & (j < ROPE_DIM)


def _epilogue_q(acc, o_ref, cos_ref, sin_ref):
    cos, sin = cos_ref[...], sin_ref[...]
    lo, hi = _rope_masks(acc.shape[0])
    scale = HEAD_DIM ** -0.5 * 1.4426950408889634
    for s in range(acc.shape[1] // LANES):
        sl = slice(s * LANES, (s + 1) * LANES)
        o_ref[:, sl] = (_rope_slab(acc[:, sl], cos, sin, lo, hi) * scale).astype(o_ref.dtype)


def _epilogue_kv(acc, o_ref, cos_ref, sin_ref):
    cos, sin = cos_ref[...], sin_ref[...]
    lo, hi = _rope_masks(acc.shape[0])
    for s in range(acc.shape[1] // LANES):
        sl = slice(s * LANES, (s + 1) * LANES)
        a = acc[:, sl]
        if s * LANES < KV_WIDTH:
            a = _rope_slab(a, cos, sin, lo, hi)
        o_ref[:, sl] = a.astype(o_ref.dtype)


def _sigmoid(x):
    return 0.5 * jnp.tanh(0.5 * x) + 0.5


def _silu(x):
    half = 0.5 * x
    return half * jnp.tanh(half) + half


def _epilogue_act(acc, o_ref, *, act):
    for s in range(acc.shape[1] // LANES):
        sl = slice(s * LANES, (s + 1) * LANES)
        o_ref[:, sl] = act(acc[:, sl]).astype(o_ref.dtype)


def _epilogue_up_attn(acc, o_ref, sa_ref):
    for s in range(acc.shape[1] // LANES):
        sl = slice(s * LANES, (s + 1) * LANES)
        o_ref[:, sl] = (sa_ref[:, sl].astype(F32) * acc[:, sl]).astype(o_ref.dtype)


def _epilogue_up_gmlp(acc, o_ref, sb_ref, ya_ref):
    for s in range(acc.shape[1] // LANES):
        sl = slice(s * LANES, (s + 1) * LANES)
        merged = ya_ref[:, sl].astype(F32) + sb_ref[:, sl].astype(F32) * acc[:, sl]
        o_ref[:, sl] = merged.astype(o_ref.dtype)


def _matmul_kernel(a_ref, w_hbm, *rest, epilogue, n_extra, w_block, nb, mb):
    extras, o_ref = rest[:n_extra], rest[n_extra]
    wb_ref, stage_ref, sem = rest[n_extra + 1:]
    n, m = pl.program_id(0), pl.program_id(1)
    k, tn = wb_ref.shape[1:]
    kc = k // mb
    cur = n % 2

    def chunk_copy(block, c, slot):
        col = pl.multiple_of(w_block(block) * tn, tn)
        return pltpu.make_async_copy(w_hbm.at[pl.ds(c * kc, kc), pl.ds(col, tn)], stage_ref.at[slot], sem.at[slot])

    @pl.when((n == 0) & (m == 0))
    def _():
        chunk_copy(0, 0, 0).start()
        for c in range(mb):
            if c + 1 < mb:
                chunk_copy(0, c + 1, (c + 1) % 2).start()
            chunk_copy(0, c, c % 2).wait()
            wb_ref[0, c * kc:(c + 1) * kc, :] = stage_ref[c % 2].astype(BF16)

    @pl.when(n + 1 < nb)
    def _():
        chunk_copy(n + 1, m, m % 2).start()

    @pl.when((n + 1 < nb) & (m > 0))
    def _():
        chunk_copy(n + 1, m - 1, (m - 1) % 2).wait()
        row = pl.multiple_of((m - 1) * kc, kc)
        wb_ref[1 - cur, pl.ds(row, kc), :] = stage_ref[(m - 1) % 2].astype(BF16)

    @pl.when((n > 0) & (m == 0))
    def _():
        chunk_copy(n, mb - 1, (mb - 1) % 2).wait()
        wb_ref[cur, (mb - 1) * kc:, :] = stage_ref[(mb - 1) % 2].astype(BF16)

    acc = jnp.dot(a_ref[...], wb_ref[cur], preferred_element_type=F32)
    epilogue(acc, o_ref, *extras)


def _matmul(name, a, w, epilogue, col_ranges, width, tm, tn, lane_tiles=(), tiles=()):
    t, k = a.shape
    per_range = width // tn
    nb, mb = per_range * len(col_ranges), t // tm
    starts = [c // tn for c in col_ranges]

    def w_block(n):
        blk = n % per_range + starts[0]
        for r in range(1, len(starts)):
            blk = jnp.where(n // per_range == r, n % per_range + starts[r], blk)
        return blk

    extra_specs = [pl.BlockSpec((tm, LANES), lambda n, m: (m, 0)) for _ in lane_tiles]
    extra_specs += [pl.BlockSpec((tm, tn), functools.partial(lambda n, m, off: (m, n + off), off=off))
                    for _, off in tiles]
    extras = list(lane_tiles) + [arr for arr, _ in tiles]
    body = functools.partial(_matmul_kernel, epilogue=epilogue, n_extra=len(extras), w_block=w_block, nb=nb, mb=mb)
    return pl.pallas_call(
        body,
        grid=(nb, mb),
        in_specs=[pl.BlockSpec((tm, k), lambda n, m: (m, 0)),
                  pl.BlockSpec(memory_space=pl.ANY)] + extra_specs,
        out_specs=pl.BlockSpec((tm, tn), lambda n, m: (m, n)),
        out_shape=jax.ShapeDtypeStruct((t, nb * tn), BF16),
        scratch_shapes=[pltpu.VMEM((2, k, tn), BF16), pltpu.VMEM((2, k // mb, tn), F32),
                        pltpu.SemaphoreType.DMA((2,))],
        compiler_params=_params("arbitrary", "arbitrary"),
        name=name,
    )(a, w, *extras)


def _attn_body(sink_ref, q_ref, kvp_ref, kvc_ref, ga_ref, o_ref, mprev_ref, mcur_ref, has_prev):
    pairs = Q_PER_KV // 2
    width = pairs * LANES
    key = lax.broadcasted_iota(jnp.int32, (BLOCK, width), 0)
    qry = lax.broadcasted_iota(jnp.int32, (BLOCK, width), 1) % LANES
    from_prev = key > qry
    left = lax.broadcasted_iota(jnp.int32, (BLOCK, LANES), 1) < HEAD_DIM
    lane_pair = lax.broadcasted_iota(jnp.int32, (1, width), 1) // LANES
    n_keys = (2 if has_prev else 1) * BLOCK
    zeros_t = jnp.zeros((HEAD_DIM, n_keys), F32)
    nt = (((1,), (1,)), ((), ()))
    log2e = 1.4426950408889634

    def head_halves(slab, e):
        swap = pltpu.roll(slab, HEAD_DIM, axis=1)
        zero = jnp.zeros_like(slab)
        if e == 0:
            return jnp.where(left, slab, zero).astype(BF16), jnp.where(left, zero, swap).astype(BF16)
        return jnp.where(left, swap, zero).astype(BF16), jnp.where(left, zero, slab).astype(BF16)

    def softmax_t(s_prev, s_cur, sink):
        s = jnp.where(from_prev, s_prev if has_prev else -jnp.inf, s_cur)
        m = jnp.maximum(jnp.max(s, axis=0, keepdims=True), sink)
        p = jnp.exp2(s - m)
        denom = jnp.sum(p, axis=0, keepdims=True) + jnp.exp2(sink - m)
        p = p.astype(BF16)
        parts = ([p * mprev_ref[...]] if has_prev else []) + [p * mcur_ref[...]]
        return parts, 1.0 / denom

    for j in range(N_KV_HEADS // 2):
        kcol, vcol = j * LANES, KV_WIDTH + j * LANES
        k_cur = kvc_ref[:, kcol:kcol + LANES].astype(F32)
        v_rows = [kvc_ref[:, vcol:vcol + LANES]]
        if has_prev:
            k_prev = kvp_ref[:, kcol:kcol + LANES].astype(F32)
            v_rows = [kvp_ref[:, vcol:vcol + LANES]] + v_rows
        v_t = jnp.concatenate(v_rows, axis=0).astype(F32).T
        for e in range(2):
            kv_head = 2 * j + e
            kc_l, kc_r = head_halves(k_cur, e)
            if has_prev:
                kp_l, kp_r = head_halves(k_prev, e)
                k_rows = jnp.concatenate([kp_l, kc_l, kp_r, kc_r], axis=0)
            else:
                k_rows = jnp.concatenate([kc_l, kc_r], axis=0)
            vt = v_t[e * HEAD_DIM:(e + 1) * HEAD_DIM]
            v_bd = jnp.concatenate([jnp.concatenate([vt, zeros_t], axis=1),
                                    jnp.concatenate([zeros_t, vt], axis=1)], axis=0).astype(BF16)
            slabs = [slice((kv_head * pairs + pp) * LANES, (kv_head * pairs + pp + 1) * LANES) for pp in range(pairs)]
            q_rows = jnp.concatenate([q_ref[:, sl] for sl in slabs], axis=0)
            sink_l = jnp.zeros((1, width), F32)
            sink_r = jnp.zeros((1, width), F32)
            for pp in range(pairs):
                head = 2 * (kv_head * pairs + pp)
                sink_l = jnp.where(lane_pair == pp, sink_ref[head] * log2e, sink_l)
                sink_r = jnp.where(lane_pair == pp, sink_ref[head + 1] * log2e, sink_r)
            s = lax.dot_general(k_rows, q_rows, nt, preferred_element_type=F32)
            tiles = [s[i * BLOCK:(i + 1) * BLOCK] for i in range(s.shape[0] // BLOCK)]
            if has_prev:
                p_l, r_l = softmax_t(tiles[0], tiles[1], sink_l)
                p_r, r_r = softmax_t(tiles[2], tiles[3], sink_r)
            else:
                p_l, r_l = softmax_t(None, tiles[0], sink_l)
                p_r, r_r = softmax_t(None, tiles[1], sink_r)
            p2 = jnp.concatenate(p_l + p_r, axis=0)
            o_t = jnp.dot(v_bd, p2, preferred_element_type=F32)
            o_t = jnp.concatenate([o_t[:HEAD_DIM] * r_l, o_t[HEAD_DIM:] * r_r], axis=0)
            for pp, sl in enumerate(slabs):
                o = o_t[:, pp * LANES:(pp + 1) * LANES].T
                o_ref[:, sl] = (o * ga_ref[:, sl].astype(F32)).astype(o_ref.dtype)


def _attn_kernel(sink_ref, q_ref, kvp_ref, kvc_ref, ga_ref, o_ref, mprev_ref, mcur_ref):
    assert WINDOW == BLOCK
    blk = pl.program_id(0) % (SEQ // BLOCK)
    width = mprev_ref.shape[1]
    key = lax.broadcasted_iota(jnp.int32, (BLOCK, width), 0)
    qry = lax.broadcasted_iota(jnp.int32, (BLOCK, width), 1) % LANES
    mprev_ref[...] = (key > qry).astype(BF16)
    mcur_ref[...] = (key <= qry).astype(BF16)
    args = (sink_ref, q_ref, kvp_ref, kvc_ref, ga_ref, o_ref, mprev_ref, mcur_ref)
    pl.when(blk > 0)(lambda: _attn_body(*args, has_prev=True))
    pl.when(blk == 0)(lambda: _attn_body(*args, has_prev=False))


def _attention(q, kv, act_a, sink):
    t = q.shape[0]
    nb = t // BLOCK
    return pl.pallas_call(
        _attn_kernel,
        grid_spec=pltpu.PrefetchScalarGridSpec(
            num_scalar_prefetch=1,
            grid=(nb,),
            in_specs=[pl.BlockSpec((BLOCK, ATTN_WIDTH), lambda i, s: (i, 0)),
                      pl.BlockSpec((BLOCK, 2 * KV_WIDTH), lambda i, s: (jnp.maximum(i - 1, 0), 0)),
                      pl.BlockSpec((BLOCK, 2 * KV_WIDTH), lambda i, s: (i, 0)),
                      pl.BlockSpec((BLOCK, ATTN_WIDTH), lambda i, s: (i, 0))],
            out_specs=pl.BlockSpec((BLOCK, ATTN_WIDTH), lambda i, s: (i, 0)),
            scratch_shapes=[pltpu.VMEM((BLOCK, (Q_PER_KV // 2) * LANES), BF16)] * 2,
        ),
        out_shape=jax.ShapeDtypeStruct((t, ATTN_WIDTH), BF16),
        compiler_params=_params("arbitrary"),
        name="swa_sink_attention",
    )(sink, q, kv, kv, act_a)


def _gating_kernel(u_ref, v_ref, gb_ref, lng_ref, lnb_ref, ws_ref, bt_ref, o_ref):
    v = v_ref[...].astype(F32)
    mu = jnp.mean(v, axis=-1, keepdims=True)
    vc = v - mu
    var = jnp.mean(vc * vc, axis=-1, keepdims=True)
    vn = (vc * lax.rsqrt(var + LN_EPS) * lng_ref[...] + lnb_ref[...]).astype(BF16)
    ti = lax.broadcasted_iota(jnp.int32, (GMLP_CHUNK, GMLP_CHUNK), 0)
    si = lax.broadcasted_iota(jnp.int32, (GMLP_CHUNK, GMLP_CHUNK), 1)
    causal = si <= ti
    bt = bt_ref[...]
    for g in range(GMLP_GROUPS):
        w = jnp.where(causal, ws_ref[g], 0.0).astype(BF16)
        bias = bt[:, g:g + 1]
        cols = slice(g * GMLP_GROUP_DIM, (g + 1) * GMLP_GROUP_DIM)
        for c in range(v.shape[0] // GMLP_CHUNK):
            rows = slice(c * GMLP_CHUNK, (c + 1) * GMLP_CHUNK)
            mixed = jnp.dot(w, vn[rows, cols], preferred_element_type=F32) + bias
            sg = u_ref[rows, cols].astype(F32) * mixed
            o_ref[rows, cols] = (sg * gb_ref[rows, cols].astype(F32)).astype(o_ref.dtype)


def _spatial_gating(act_uv, act_gate, ln_g, ln_b, w_s, b_t):
    t = act_uv.shape[0]
    r = 2 * GMLP_CHUNK
    w = GMLP_WIDTH
    return pl.pallas_call(
        _gating_kernel,
        grid=(t // r,),
        in_specs=[pl.BlockSpec((r, w), lambda i: (i, 0)),
                  pl.BlockSpec((r, w), lambda i: (i, 1)),
                  pl.BlockSpec((r, w), lambda i: (i, 1)),
                  pl.BlockSpec((1, w), lambda i: (0, 0)),
                  pl.BlockSpec((1, w), lambda i: (0, 0)),
                  pl.BlockSpec((GMLP_GROUPS, GMLP_CHUNK, GMLP_CHUNK), lambda i: (0, 0, 0)),
                  pl.BlockSpec((GMLP_CHUNK, GMLP_GROUPS), lambda i: (0, 0))],
        out_specs=pl.BlockSpec((r, w), lambda i: (i, 0)),
        out_shape=jax.ShapeDtypeStruct((t, w), BF16),
        compiler_params=_params("arbitrary"),
        name="spatial_gating",
    )(act_uv, act_uv, act_gate, ln_g, ln_b, w_s, b_t)


def _out_kernel(a_ref, w_ref, x_ref, g_ref, o_ref, y0_ref, y1_ref, ss0_ref, ss1_ref):
    i, j = pl.program_id(0), pl.program_id(1)
    col = pl.multiple_of(j * TN, TN)

    @pl.when((i == 0) & (j == 0))
    def _():
        y1_ref[...] = jnp.zeros_like(y1_ref)
        ss1_ref[...] = jnp.zeros_like(ss1_ref)

    def step(y_ref, ss_ref, yp_ref, ssp_ref):
        ms_prev = jnp.sum(ssp_ref[...], axis=-1, keepdims=True) * (1.0 / D_MODEL)
        o_ref[...] = yp_ref[:, pl.ds(col, TN)] * lax.rsqrt(ms_prev + NORM_EPS) * g_ref[:, pl.ds(col, TN)]
        y = x_ref[...] + jnp.dot(a_ref[...], w_ref[...], preferred_element_type=F32)
        sq = y * y
        part = sq[:, :LANES]
        for s in range(1, TN // LANES):
            part = part + sq[:, s * LANES:(s + 1) * LANES]
        ss_ref[...] = jnp.where(j == 0, part, ss_ref[...] + part)
        y_ref[:, pl.ds(col, TN)] = y

    pl.when(i % 2 == 0)(lambda: step(y0_ref, ss0_ref, y1_ref, ss1_ref))
    pl.when(i % 2 == 1)(lambda: step(y1_ref, ss1_ref, y0_ref, ss0_ref))


def _out_proj(a, w, x2, g_row):
    t, k = a.shape
    n = w.shape[1]
    last = t // TM_OUT - 1
    return pl.pallas_call(
        _out_kernel,
        grid=(t // TM_OUT + 1, n // TN),
        in_specs=[pl.BlockSpec((TM_OUT, k), lambda i, j: (jnp.minimum(i, last), 0)),
                  pl.BlockSpec((k, TN), lambda i, j: (0, j)),
                  pl.BlockSpec((TM_OUT, TN), lambda i, j: (jnp.minimum(i, last), j)),
                  pl.BlockSpec((1, n), lambda i, j: (0, 0))],
        out_specs=pl.BlockSpec((TM_OUT, TN), lambda i, j: (jnp.maximum(i - 1, 0), jnp.where(i == 0, 0, j))),
        out_shape=jax.ShapeDtypeStruct((t, n), F32),
        scratch_shapes=[pltpu.VMEM((TM_OUT, n), F32), pltpu.VMEM((TM_OUT, n), F32),
                        pltpu.VMEM((TM_OUT, LANES), F32), pltpu.VMEM((TM_OUT, LANES), F32)],
        compiler_params=_params("arbitrary", "arbitrary"),
        name="out_proj_residual_norm",
    )(a, w, x2, g_row)


def _rope_inv_freq_lanes():
    inv_freq = ROPE_THETA ** (-jnp.arange(ROPE_HALF, dtype=F32) * 2.0 / ROPE_DIM)
    j = jnp.arange(LANES) % HEAD_DIM
    return jnp.where(j < ROPE_DIM, inv_freq[j % ROPE_HALF], 0.0).astype(F32)[None, :]


def kernel(x, positions, norm_g, w_in, attn_sink, gmlp_ln_g, gmlp_ln_b, w_spatial, b_spatial, w_up_attn,
           w_up_gmlp, w_out, final_norm_g):
    b, s, d = x.shape
    t = b * s
    assert s == SEQ and d == D_MODEL and norm_g.shape[0] == DEPTH == 1
    x2 = x.reshape(t, d)
    cos_t, sin_t = _rope_tables(positions.reshape(t, 1), _rope_inv_freq_lanes())
    nw = D_MODEL // TN
    h = _rmsnorm(x2, norm_g)
    w_i = w_in[0]
    q = _matmul("in_proj_q", h, w_i, _epilogue_q, [OFF_Q], ATTN_WIDTH, TM, TN, lane_tiles=(cos_t, sin_t))
    kv = _matmul("in_proj_kv", h, w_i, _epilogue_kv, [OFF_KV], 2 * KV_WIDTH, TM, TN, lane_tiles=(cos_t, sin_t))
    act_gate = _matmul("in_proj_silu", h, w_i, functools.partial(_epilogue_act, act=_silu),
                       [OFF_GATE_A, OFF_GATE_B], ATTN_WIDTH, TM, TN)
    act_uv = _matmul("in_proj_gelu", h, w_i, functools.partial(_epilogue_act, act=jax.nn.gelu),
                     [OFF_U], 2 * GMLP_WIDTH, TM, TN)
    act_merge = _matmul("in_proj_sigmoid", h, w_i, functools.partial(_epilogue_act, act=_sigmoid),
                        [OFF_MERGE], 2 * D_MODEL, TM, TN)
    attn_g = _attention(q, kv, act_gate, attn_sink[0])
    ya = _matmul("up_attn", attn_g, w_up_attn[0], _epilogue_up_attn, [0], D_MODEL, TM, TN, tiles=[(act_merge, 0)])
    sgb = _spatial_gating(act_uv, act_gate, gmlp_ln_g, gmlp_ln_b, w_spatial[0], b_spatial[0].T)
    merged = _matmul("up_gmlp_merge", sgb, w_up_gmlp[0], _epilogue_up_gmlp, [0], D_MODEL, TM, TN,
                     tiles=[(act_merge, nw), (ya, 0)])
    out = _out_proj(merged, w_out[0].astype(BF16), x2, final_norm_g[None, :])
    return out.reshape(b, s, d)
```

```python
import functools

import jax
import jax.numpy as jnp
from jax import lax
from jax.experimental import pallas as pl
from jax.experimental.pallas import tpu as pltpu

D_MODEL = 4096
SEQ = 4096
DEPTH = 1
N_Q_HEADS = 64
N_KV_HEADS = 8
HEAD_DIM = 64
Q_PER_KV = N_Q_HEADS // N_KV_HEADS
ATTN_WIDTH = N_Q_HEADS * HEAD_DIM
KV_WIDTH = N_KV_HEADS * HEAD_DIM
WINDOW = 128
BLOCK = 128
ROPE_THETA = 500000.0
ROPE_DIM = HEAD_DIM // 4
ROPE_HALF = ROPE_DIM // 2
GMLP_WIDTH = D_MODEL
GMLP_GROUPS = 8
GMLP_GROUP_DIM = GMLP_WIDTH // GMLP_GROUPS
GMLP_CHUNK = 128
NORM_EPS = 1e-5
LN_EPS = 1e-5

OFF_Q = 0
OFF_KV = ATTN_WIDTH
OFF_GATE_A = OFF_KV + 2 * KV_WIDTH
OFF_U = OFF_GATE_A + ATTN_WIDTH
OFF_GATE_B = OFF_U + 2 * GMLP_WIDTH
OFF_MERGE = OFF_GATE_B + GMLP_WIDTH

LANES = 128
VMEM_LIMIT = 56 * 1024 * 1024

TM = 1024
TN = 1024
M_CHUNKS = 4
TM_NORM = 512
NORM_CHUNKS = 2
TM_OUT = 512

BF16 = jnp.bfloat16
F32 = jnp.float32


def _params(*sem):
    return pltpu.CompilerParams(dimension_semantics=sem, vmem_limit_bytes=VMEM_LIMIT)


def _rope_table_kernel(pos_ref, invf_ref, cos_ref, sin_ref):
    ang = pos_ref[...].astype(F32) * invf_ref[...]
    cos_ref[...] = jnp.cos(ang)
    sin_ref[...] = jnp.sin(ang)


def _rope_tables(pos_col, invf_lanes):
    t = pos_col.shape[0]
    r = 2048
    return pl.pallas_call(
        _rope_table_kernel,
        grid=(t // r,),
        in_specs=[pl.BlockSpec((r, 1), lambda i: (i, 0)),
                  pl.BlockSpec((1, LANES), lambda i: (0, 0))],
        out_specs=[pl.BlockSpec((r, LANES), lambda i: (i, 0)),
                   pl.BlockSpec((r, LANES), lambda i: (i, 0))],
        out_shape=[jax.ShapeDtypeStruct((t, LANES), F32)] * 2,
        compiler_params=_params("arbitrary"),
        name="rope_tables",
    )(pos_col, invf_lanes)


def _rope_slab(a, cos, sin, lo, hi):
    up = pltpu.roll(a, LANES - ROPE_HALF, axis=1)
    dn = pltpu.roll(a, ROPE_HALF, axis=1)
    return jnp.where(lo, a * cos - up * sin, jnp.where(hi, a * cos + dn * sin, a))


def _rope_masks(rows):
    j = lax.broadcasted_iota(jnp.int32, (rows, LANES), 1) & (HEAD_DIM - 1)
    return j < ROPE_HALF, (j >= ROPE_HALF) & (j < ROPE_DIM)


def _epilogue_q(acc, o_ref, cos_ref, sin_ref):
    cos, sin = cos_ref[...], sin_ref[...]
    lo, hi = _rope_masks(acc.shape[0])
    scale = HEAD_DIM ** -0.5 * 1.4426950408889634
    for s in range(acc.shape[1] // LANES):
        sl = slice(s * LANES, (s + 1) * LANES)
        o_ref[:, sl] = (_rope_slab(acc[:, sl], cos, sin, lo, hi) * scale).astype(o_ref.dtype)


def _epilogue_kv(acc, o_ref, cos_ref, sin_ref):
    cos, sin = cos_ref[...], sin_ref[...]
    lo, hi = _rope_masks(acc.shape[0])
    for s in range(acc.shape[1] // LANES):
        sl = slice(s * LANES, (s + 1) * LANES)
        a = acc[:, sl]
        if s * LANES < KV_WIDTH:
            a = _rope_slab(a, cos, sin, lo, hi)
        o_ref[:, sl] = a.astype(o_ref.dtype)


def _sigmoid(x):
    return 0.5 * jnp.tanh(0.5 * x) + 0.5


def _silu(x):
    half = 0.5 * x
    return half * jnp.tanh(half) + half


def _epilogue_act(acc, o_ref, *, act):
    for s in range(acc.shape[1] // LANES):
        sl = slice(s * LANES, (s + 1) * LANES)
        o_ref[:, sl] = act(acc[:, sl]).astype(o_ref.dtype)


def _epilogue_up_attn(acc, o_ref, sa_ref):
    for s in range(acc.shape[1] // LANES):
        sl = slice(s * LANES, (s + 1) * LANES)
        o_ref[:, sl] = (sa_ref[:, sl].astype(F32) * acc[:, sl]).astype(o_ref.dtype)


def _epilogue_up_gmlp(acc, o_ref, sb_ref, ya_ref):
    for s in range(acc.shape[1] // LANES):
        sl = slice(s * LANES, (s + 1) * LANES)
        merged = ya_ref[:, sl].astype(F32) + sb_ref[:, sl].astype(F32) * acc[:, sl]
        o_ref[:, sl] = merged.astype(o_ref.dtype)


def _matmul_kernel(a_ref, w_hbm, *rest, epilogue, n_extra, w_block, nb, mb, m_chunks, with_norm):
    if with_norm:
        g_ref, rest = rest[0], rest[1:]
    extras, o_ref = rest[:n_extra], rest[n_extra]
    rest = rest[n_extra + 1:]
    if with_norm:
        h_ref, rest = rest[0], rest[1:]
    wb_ref, stage_ref, sem = rest
    n, m = pl.program_id(0), pl.program_id(1)
    k, tn = wb_ref.shape[1:]
    kc = k // mb
    cur = n % 2

    def chunk_copy(block, c, slot):
        col = pl.multiple_of(w_block(block) * tn, tn)
        return pltpu.make_async_copy(w_hbm.at[pl.ds(c * kc, kc), pl.ds(col, tn)], stage_ref.at[slot], sem.at[slot])

    @pl.when((n == 0) & (m == 0))
    def _():
        chunk_copy(0, 0, 0).start()
        for c in range(mb):
            if c + 1 < mb:
                chunk_copy(0, c + 1, (c + 1) % 2).start()
            chunk_copy(0, c, c % 2).wait()
            wb_ref[0, c * kc:(c + 1) * kc, :] = stage_ref[c % 2].astype(BF16)

    if nb > 1:
        @pl.when(n + 1 < nb)
        def _():
            chunk_copy(n + 1, m, m % 2).start()

        @pl.when((n + 1 < nb) & (m > 0))
        def _():
            chunk_copy(n + 1, m - 1, (m - 1) % 2).wait()
            row = pl.multiple_of((m - 1) * kc, kc)
            wb_ref[1 - cur, pl.ds(row, kc), :] = stage_ref[(m - 1) % 2].astype(BF16)

        @pl.when((n > 0) & (m == 0))
        def _():
            chunk_copy(n, mb - 1, (mb - 1) % 2).wait()
            wb_ref[cur, (mb - 1) * kc:, :] = stage_ref[(mb - 1) % 2].astype(BF16)

    rows = a_ref.shape[0] // m_chunks
    for c in range(m_chunks):
        rs = pl.ds(c * rows, rows)
        lhs = a_ref[rs, :]
        if with_norm:
            ms = jnp.mean(lhs * lhs, axis=-1, keepdims=True)
            lhs = (lhs * lax.rsqrt(ms + NORM_EPS) * g_ref[...]).astype(BF16)
            h_ref[rs, :] = lhs
        acc = jnp.dot(lhs, wb_ref[cur], preferred_element_type=F32)
        epilogue(acc, o_ref.at[rs, :], *[e.at[rs, :] for e in extras])


def _matmul(name, a, w, epilogue, col_ranges, width, tm, tn, lane_tiles=(), tiles=(), m_chunks=M_CHUNKS,
            norm_gain=None):
    t, k = a.shape
    per_range = width // tn
    nb, mb = per_range * len(col_ranges), t // tm
    starts = [c // tn for c in col_ranges]
    with_norm = norm_gain is not None
    assert not with_norm or nb == 1

    def w_block(n):
        blk = n % per_range + starts[0]
        for r in range(1, len(starts)):
            blk = jnp.where(n // per_range == r, n % per_range + starts[r], blk)
        return blk

    extra_specs = [pl.BlockSpec((tm, LANES), lambda n, m: (m, 0)) for _ in lane_tiles]
    extra_specs += [pl.BlockSpec((tm, tn), functools.partial(lambda n, m, off: (m, n + off), off=off))
                    for _, off in tiles]
    extras = list(lane_tiles) + [arr for arr, _ in tiles]
    body = functools.partial(_matmul_kernel, epilogue=epilogue, n_extra=len(extras), w_block=w_block, nb=nb, mb=mb,
                             m_chunks=m_chunks, with_norm=with_norm)
    in_specs = [pl.BlockSpec((tm, k), lambda n, m: (m, 0)), pl.BlockSpec(memory_space=pl.ANY)]
    out_specs = [pl.BlockSpec((tm, tn), lambda n, m: (m, n))]
    out_shape = [jax.ShapeDtypeStruct((t, nb * tn), BF16)]
    operands = [a, w]
    if with_norm:
        in_specs.append(pl.BlockSpec((1, k), lambda n, m: (0, 0)))
        operands.append(norm_gain)
        out_specs.append(pl.BlockSpec((tm, k), lambda n, m: (m, 0)))
        out_shape.append(jax.ShapeDtypeStruct((t, k), BF16))
    outs = pl.pallas_call(
        body,
        grid=(nb, mb),
        in_specs=in_specs + extra_specs,
        out_specs=out_specs,
        out_shape=out_shape,
        scratch_shapes=[pltpu.VMEM((min(nb, 2), k, tn), BF16), pltpu.VMEM((2, k // mb, tn), F32),
                        pltpu.SemaphoreType.DMA((2,))],
        compiler_params=_params("arbitrary", "arbitrary"),
        name=name,
    )(*operands, *extras)
    return outs if with_norm else outs[0]


def _attn_body(sink_ref, q_ref, kvp_ref, kvc_ref, ga_ref, o_ref, mprev_ref, mcur_ref, has_prev):
    pairs = Q_PER_KV // 2
    width = pairs * LANES
    key = lax.broadcasted_iota(jnp.int32, (BLOCK, width), 0)
    qry = lax.broadcasted_iota(jnp.int32, (BLOCK, width), 1) % LANES
    from_prev = key > qry
    left = lax.broadcasted_iota(jnp.int32, (BLOCK, LANES), 1) < HEAD_DIM
    lane_pair = lax.broadcasted_iota(jnp.int32, (1, width), 1) // LANES
    n_keys = (2 if has_prev else 1) * BLOCK
    zeros_t = jnp.zeros((HEAD_DIM, n_keys), F32)
    nt = (((1,), (1,)), ((), ()))
    log2e = 1.4426950408889634

    def head_halves(slab, e):
        swap = pltpu.roll(slab, HEAD_DIM, axis=1)
        zero = jnp.zeros_like(slab)
        if e == 0:
            return jnp.where(left, slab, zero).astype(BF16), jnp.where(left, zero, swap).astype(BF16)
        return jnp.where(left, swap, zero).astype(BF16), jnp.where(left, zero, slab).astype(BF16)

    def softmax_t(s_prev, s_cur, sink):
        s = jnp.where(from_prev, s_prev if has_prev else -jnp.inf, s_cur)
        m = jnp.maximum(jnp.max(s, axis=0, keepdims=True), sink)
        p = jnp.exp2(s - m)
        denom = jnp.sum(p, axis=0, keepdims=True) + jnp.exp2(sink - m)
        p = p.astype(BF16)
        parts = ([p * mprev_ref[...]] if has_prev else []) + [p * mcur_ref[...]]
        return parts, 1.0 / denom

    for j in range(N_KV_HEADS // 2):
        kcol, vcol = j * LANES, KV_WIDTH + j * LANES
        k_cur = kvc_ref[:, kcol:kcol + LANES].astype(F32)
        v_rows = [kvc_ref[:, vcol:vcol + LANES]]
        if has_prev:
            k_prev = kvp_ref[:, kcol:kcol + LANES].astype(F32)
            v_rows = [kvp_ref[:, vcol:vcol + LANES]] + v_rows
        v_t = jnp.concatenate(v_rows, axis=0).astype(F32).T
        for e in range(2):
            kv_head = 2 * j + e
            kc_l, kc_r = head_halves(k_cur, e)
            if has_prev:
                kp_l, kp_r = head_halves(k_prev, e)
                k_rows = jnp.concatenate([kp_l, kc_l, kp_r, kc_r], axis=0)
            else:
                k_rows = jnp.concatenate([kc_l, kc_r], axis=0)
            vt = v_t[e * HEAD_DIM:(e + 1) * HEAD_DIM]
            v_bd = jnp.concatenate([jnp.concatenate([vt, zeros_t], axis=1),
                                    jnp.concatenate([zeros_t, vt], axis=1)], axis=0).astype(BF16)
            slabs = [slice((kv_head * pairs + pp) * LANES, (kv_head * pairs + pp + 1) * LANES) for pp in range(pairs)]
            q_rows = jnp.concatenate([q_ref[:, sl] for sl in slabs], axis=0)
            sink_l = jnp.zeros((1, width), F32)
            sink_r = jnp.zeros((1, width), F32)
            for pp in range(pairs):
                head = 2 * (kv_head * pairs + pp)
                sink_l = jnp.where(lane_pair == pp, sink_ref[head] * log2e, sink_l)
                sink_r = jnp.where(lane_pair == pp, sink_ref[head + 1] * log2e, sink_r)
            s = lax.dot_general(k_rows, q_rows, nt, preferred_element_type=F32)
            tiles = [s[i * BLOCK:(i + 1) * BLOCK] for i in range(s.shape[0] // BLOCK)]
            if has_prev:
                p_l, r_l = softmax_t(tiles[0], tiles[1], sink_l)
                p_r, r_r = softmax_t(tiles[2], tiles[3], sink_r)
            else:
                p_l, r_l = softmax_t(None, tiles[0], sink_l)
                p_r, r_r = softmax_t(None, tiles[1], sink_r)
            p2 = jnp.concatenate(p_l + p_r, axis=0)
            o_t = jnp.dot(v_bd, p2, preferred_element_type=F32)
            o_t = jnp.concatenate([o_t[:HEAD_DIM] * r_l, o_t[HEAD_DIM:] * r_r], axis=0)
            for pp, sl in enumerate(slabs):
                o = o_t[:, pp * LANES:(pp + 1) * LANES].T
                o_ref[:, sl] = (o * ga_ref[:, sl].astype(F32)).astype(o_ref.dtype)


def _attn_kernel(sink_ref, q_ref, kvp_ref, kvc_ref, ga_ref, o_ref, mprev_ref, mcur_ref):
    assert WINDOW == BLOCK
    blk = pl.program_id(0) % (SEQ // BLOCK)
    width = mprev_ref.shape[1]
    key = lax.broadcasted_iota(jnp.int32, (BLOCK, width), 0)
    qry = lax.broadcasted_iota(jnp.int32, (BLOCK, width), 1) % LANES
    mprev_ref[...] = (key > qry).astype(BF16)
    mcur_ref[...] = (key <= qry).astype(BF16)
    args = (sink_ref, q_ref, kvp_ref, kvc_ref, ga_ref, o_ref, mprev_ref, mcur_ref)
    pl.when(blk > 0)(lambda: _attn_body(*args, has_prev=True))
    pl.when(blk == 0)(lambda: _attn_body(*args, has_prev=False))


def _attention(q, kv, act_a, sink):
    t = q.shape[0]
    nb = t // BLOCK
    return pl.pallas_call(
        _attn_kernel,
        grid_spec=pltpu.PrefetchScalarGridSpec(
            num_scalar_prefetch=1,
            grid=(nb,),
            in_specs=[pl.BlockSpec((BLOCK, ATTN_WIDTH), lambda i, s: (i, 0)),
                      pl.BlockSpec((BLOCK, 2 * KV_WIDTH), lambda i, s: (jnp.maximum(i - 1, 0), 0)),
                      pl.BlockSpec((BLOCK, 2 * KV_WIDTH), lambda i, s: (i, 0)),
                      pl.BlockSpec((BLOCK, ATTN_WIDTH), lambda i, s: (i, 0))],
            out_specs=pl.BlockSpec((BLOCK, ATTN_WIDTH), lambda i, s: (i, 0)),
            scratch_shapes=[pltpu.VMEM((BLOCK, (Q_PER_KV // 2) * LANES), BF16)] * 2,
        ),
        out_shape=jax.ShapeDtypeStruct((t, ATTN_WIDTH), BF16),
        compiler_params=_params("arbitrary"),
        name="swa_sink_attention",
    )(sink, q, kv, kv, act_a)


def _gating_kernel(u_ref, v_ref, gb_ref, lng_ref, lnb_ref, ws_ref, bt_ref, o_ref):
    v = v_ref[...].astype(F32)
    mu = jnp.mean(v, axis=-1, keepdims=True)
    vc = v - mu
    var = jnp.mean(vc * vc, axis=-1, keepdims=True)
    vn = (vc * lax.rsqrt(var + LN_EPS) * lng_ref[...] + lnb_ref[...]).astype(BF16)
    ti = lax.broadcasted_iota(jnp.int32, (GMLP_CHUNK, GMLP_CHUNK), 0)
    si = lax.broadcasted_iota(jnp.int32, (GMLP_CHUNK, GMLP_CHUNK), 1)
    causal = si <= ti
    bt = bt_ref[...]
    for g in range(GMLP_GROUPS):
        w = jnp.where(causal, ws_ref[g], 0.0).astype(BF16)
        bias = bt[:, g:g + 1]
        cols = slice(g * GMLP_GROUP_DIM, (g + 1) * GMLP_GROUP_DIM)
        for c in range(v.shape[0] // GMLP_CHUNK):
            rows = slice(c * GMLP_CHUNK, (c + 1) * GMLP_CHUNK)
            mixed = jnp.dot(w, vn[rows, cols], preferred_element_type=F32) + bias
            sg = u_ref[rows, cols].astype(F32) * mixed
            o_ref[rows, cols] = (sg * gb_ref[rows, cols].astype(F32)).astype(o_ref.dtype)


def _spatial_gating(act_uv, act_gate, ln_g, ln_b, w_s, b_t):
    t = act_uv.shape[0]
    r = 2 * GMLP_CHUNK
    w = GMLP_WIDTH
    return pl.pallas_call(
        _gating_kernel,
        grid=(t // r,),
        in_specs=[pl.BlockSpec((r, w), lambda i: (i, 0)),
                  pl.BlockSpec((r, w), lambda i: (i, 1)),
                  pl.BlockSpec((r, w), lambda i: (i, 1)),
                  pl.BlockSpec((1, w), lambda i: (0, 0)),
                  pl.BlockSpec((1, w), lambda i: (0, 0)),
                  pl.BlockSpec((GMLP_GROUPS, GMLP_CHUNK, GMLP_CHUNK), lambda i: (0, 0, 0)),
                  pl.BlockSpec((GMLP_CHUNK, GMLP_GROUPS), lambda i: (0, 0))],
        out_specs=pl.BlockSpec((r, w), lambda i: (i, 0)),
        out_shape=jax.ShapeDtypeStruct((t, w), BF16),
        compiler_params=_params("arbitrary"),
        name="spatial_gating",
    )(act_uv, act_uv, act_gate, ln_g, ln_b, w_s, b_t)


def _out_kernel(a_ref, w_ref, x_ref, g_ref, o_ref, y0_ref, y1_ref, ss0_ref, ss1_ref):
    i, j = pl.program_id(0), pl.program_id(1)
    col = pl.multiple_of(j * TN, TN)

    @pl.when((i == 0) & (j == 0))
    def _():
        y1_ref[...] = jnp.zeros_like(y1_ref)
        ss1_ref[...] = jnp.zeros_like(ss1_ref)

    def step(y_ref, ss_ref, yp_ref, ssp_ref):
        ms_prev = jnp.sum(ssp_ref[...], axis=-1, keepdims=True) * (1.0 / D_MODEL)
        o_ref[...] = yp_ref[:, pl.ds(col, TN)] * lax.rsqrt(ms_prev + NORM_EPS) * g_ref[:, pl.ds(col, TN)]
        y = x_ref[...] + jnp.dot(a_ref[...], w_ref[...], preferred_element_type=F32)
        sq = y * y
        part = sq[:, :LANES]
        for s in range(1, TN // LANES):
            part = part + sq[:, s * LANES:(s + 1) * LANES]
        ss_ref[...] = jnp.where(j == 0, part, ss_ref[...] + part)
        y_ref[:, pl.ds(col, TN)] = y

    pl.when(i % 2 == 0)(lambda: step(y0_ref, ss0_ref, y1_ref, ss1_ref))
    pl.when(i % 2 == 1)(lambda: step(y1_ref, ss1_ref, y0_ref, ss0_ref))


def _out_proj(a, w, x2, g_row):
    t, k = a.shape
    n = w.shape[1]
    last = t // TM_OUT - 1
    return pl.pallas_call(
        _out_kernel,
        grid=(t // TM_OUT + 1, n // TN),
        in_specs=[pl.BlockSpec((TM_OUT, k), lambda i, j: (jnp.minimum(i, last), 0)),
                  pl.BlockSpec((k, TN), lambda i, j: (0, j)),
                  pl.BlockSpec((TM_OUT, TN), lambda i, j: (jnp.minimum(i, last), j)),
                  pl.BlockSpec((1, n), lambda i, j: (0, 0))],
        out_specs=pl.BlockSpec((TM_OUT, TN), lambda i, j: (jnp.maximum(i - 1, 0), jnp.where(i == 0, 0, j))),
        out_shape=jax.ShapeDtypeStruct((t, n), F32),
        scratch_shapes=[pltpu.VMEM((TM_OUT, n), F32), pltpu.VMEM((TM_OUT, n), F32),
                        pltpu.VMEM((TM_OUT, LANES), F32), pltpu.VMEM((TM_OUT, LANES), F32)],
        compiler_params=_params("arbitrary", "arbitrary"),
        name="out_proj_residual_norm",
    )(a, w, x2, g_row)


def _rope_inv_freq_lanes():
    inv_freq = ROPE_THETA ** (-jnp.arange(ROPE_HALF, dtype=F32) * 2.0 / ROPE_DIM)
    j = jnp.arange(LANES) % HEAD_DIM
    return jnp.where(j < ROPE_DIM, inv_freq[j % ROPE_HALF], 0.0).astype(F32)[None, :]


def kernel(x, positions, norm_g, w_in, attn_sink, gmlp_ln_g, gmlp_ln_b, w_spatial, b_spatial, w_up_attn,
           w_up_gmlp, w_out, final_norm_g):
    b, s, d = x.shape
    t = b * s
    assert s == SEQ and d == D_MODEL and norm_g.shape[0] == DEPTH == 1
    x2 = x.reshape(t, d)
    cos_t, sin_t = _rope_tables(positions.reshape(t, 1), _rope_inv_freq_lanes())
    nw = D_MODEL // TN
    w_i = w_in[0]
    kv, h = _matmul("norm_in_proj_kv", x2, w_i, _epilogue_kv, [OFF_KV], 2 * KV_WIDTH, TM_NORM, TN,
                    lane_tiles=(cos_t, sin_t), m_chunks=NORM_CHUNKS, norm_gain=norm_g)
    q = _matmul("in_proj_q", h, w_i, _epilogue_q, [OFF_Q], ATTN_WIDTH, TM, TN, lane_tiles=(cos_t, sin_t))
    act_gate = _matmul("in_proj_silu", h, w_i, functools.partial(_epilogue_act, act=_silu),
                       [OFF_GATE_A, OFF_GATE_B], ATTN_WIDTH, TM, TN)
    act_uv = _matmul("in_proj_gelu", h, w_i, functools.partial(_epilogue_act, act=jax.nn.gelu),
                     [OFF_U], 2 * GMLP_WIDTH, TM, TN)
    act_merge = _matmul("in_proj_sigmoid", h, w_i, functools.partial(_epilogue_act, act=_sigmoid),
                        [OFF_MERGE], 2 * D_MODEL, TM, TN)
    attn_g = _attention(q, kv, act_gate, attn_sink[0])
    ya = _matmul("up_attn", attn_g, w_up_attn[0], _epilogue_up_attn, [0], D_MODEL, TM, TN, tiles=[(act_merge, 0)])
    sgb = _spatial_gating(act_uv, act_gate, gmlp_ln_g, gmlp_ln_b, w_spatial[0], b_spatial[0].T)
    merged = _matmul("up_gmlp_merge", sgb, w_up_gmlp[0], _epilogue_up_gmlp, [0], D_MODEL, TM, TN,
                     tiles=[(act_merge, nw), (ya, 0)])
    out = _out_proj(merged, w_out[0].astype(BF16), x2, final_norm_g[None, :])
    return out.reshape(b, s, d)
```

```python
import functools

import jax
import jax.numpy as jnp
from jax import lax
from jax.experimental import pallas as pl
from jax.experimental.pallas import tpu as pltpu

D_MODEL = 4096
SEQ = 4096
DEPTH = 1
N_Q_HEADS = 64
N_KV_HEADS = 8
HEAD_DIM = 64
Q_PER_KV = N_Q_HEADS // N_KV_HEADS
ATTN_WIDTH = N_Q_HEADS * HEAD_DIM
KV_WIDTH = N_KV_HEADS * HEAD_DIM
WINDOW = 128
BLOCK = 128
ROPE_THETA = 500000.0
ROPE_DIM = HEAD_DIM // 4
ROPE_HALF = ROPE_DIM // 2
GMLP_WIDTH = D_MODEL
GMLP_GROUPS = 8
GMLP_GROUP_DIM = GMLP_WIDTH // GMLP_GROUPS
GMLP_CHUNK = 128
NORM_EPS = 1e-5
LN_EPS = 1e-5

OFF_Q = 0
OFF_KV = ATTN_WIDTH
OFF_GATE_A = OFF_KV + 2 * KV_WIDTH
OFF_U = OFF_GATE_A + ATTN_WIDTH
OFF_GATE_B = OFF_U + 2 * GMLP_WIDTH
OFF_MERGE = OFF_GATE_B + GMLP_WIDTH

LANES = 128
VMEM_LIMIT = 56 * 1024 * 1024

TM = 1024
TN = 1024
Q_CHUNKS = 4
TM_NORM = 512
NORM_CHUNKS = 2
TM_OUT = 512

BF16 = jnp.bfloat16
F32 = jnp.float32


def _params(*sem):
    return pltpu.CompilerParams(dimension_semantics=sem, vmem_limit_bytes=VMEM_LIMIT)


def _rope_table_kernel(pos_ref, invf_ref, cos_ref, sin_ref):
    ang = pos_ref[...].astype(F32) * invf_ref[...]
    cos_ref[...] = jnp.cos(ang)
    sin_ref[...] = jnp.sin(ang)


def _rope_tables(pos_col, invf_lanes):
    t = pos_col.shape[0]
    r = 2048
    return pl.pallas_call(
        _rope_table_kernel,
        grid=(t // r,),
        in_specs=[pl.BlockSpec((r, 1), lambda i: (i, 0)),
                  pl.BlockSpec((1, LANES), lambda i: (0, 0))],
        out_specs=[pl.BlockSpec((r, LANES), lambda i: (i, 0)),
                   pl.BlockSpec((r, LANES), lambda i: (i, 0))],
        out_shape=[jax.ShapeDtypeStruct((t, LANES), F32)] * 2,
        compiler_params=_params("arbitrary"),
        name="rope_tables",
    )(pos_col, invf_lanes)


def _rope_slab(a, cos, sin, lo, hi):
    up = pltpu.roll(a, LANES - ROPE_HALF, axis=1)
    dn = pltpu.roll(a, ROPE_HALF, axis=1)
    return jnp.where(lo, a * cos - up * sin, jnp.where(hi, a * cos + dn * sin, a))


def _rope_masks(rows):
    j = lax.broadcasted_iota(jnp.int32, (rows, LANES), 1) & (HEAD_DIM - 1)
    return j < ROPE_HALF, (j >= ROPE_HALF) & (j < ROPE_DIM)


def _epilogue_q(acc, o_ref, cos_ref, sin_ref):
    cos, sin = cos_ref[...], sin_ref[...]
    lo, hi = _rope_masks(acc.shape[0])
    scale = HEAD_DIM ** -0.5 * 1.4426950408889634
    for s in range(acc.shape[1] // LANES):
        sl = slice(s * LANES, (s + 1) * LANES)
        o_ref[:, sl] = (_rope_slab(acc[:, sl], cos, sin, lo, hi) * scale).astype(o_ref.dtype)


def _epilogue_kv(acc, o_ref, cos_ref, sin_ref):
    cos, sin = cos_ref[...], sin_ref[...]
    lo, hi = _rope_masks(acc.shape[0])
    for s in range(acc.shape[1] // LANES):
        sl = slice(s * LANES, (s + 1) * LANES)
        a = acc[:, sl]
        if s * LANES < KV_WIDTH:
            a = _rope_slab(a, cos, sin, lo, hi)
        o_ref[:, sl] = a.astype(o_ref.dtype)


def _sigmoid(x):
    return 0.5 * jnp.tanh(0.5 * x) + 0.5


def _silu(x):
    half = 0.5 * x
    return half * jnp.tanh(half) + half


def _epilogue_act(acc, o_ref, *, act):
    for s in range(acc.shape[1] // LANES):
        sl = slice(s * LANES, (s + 1) * LANES)
        o_ref[:, sl] = act(acc[:, sl]).astype(o_ref.dtype)


def _epilogue_up_attn(acc, o_ref, sa_ref):
    for s in range(acc.shape[1] // LANES):
        sl = slice(s * LANES, (s + 1) * LANES)
        o_ref[:, sl] = (sa_ref[:, sl].astype(F32) * acc[:, sl]).astype(o_ref.dtype)


def _epilogue_up_gmlp(acc, o_ref, sb_ref, ya_ref):
    for s in range(acc.shape[1] // LANES):
        sl = slice(s * LANES, (s + 1) * LANES)
        merged = ya_ref[:, sl].astype(F32) + sb_ref[:, sl].astype(F32) * acc[:, sl]
        o_ref[:, sl] = merged.astype(o_ref.dtype)


def _matmul_kernel(a_ref, w_hbm, *rest, epilogue, n_extra, w_block, nb, mb, m_chunks, with_norm):
    if with_norm:
        g_ref, rest = rest[0], rest[1:]
    extras, o_ref = rest[:n_extra], rest[n_extra]
    rest = rest[n_extra + 1:]
    if with_norm:
        h_ref, rest = rest[0], rest[1:]
    wb_ref, stage_ref, sem = rest
    n, m = pl.program_id(0), pl.program_id(1)
    k, tn = wb_ref.shape[1:]
    kc = k // mb
    cur = n % 2

    def chunk_copy(block, c, slot):
        col = pl.multiple_of(w_block(block) * tn, tn)
        return pltpu.make_async_copy(w_hbm.at[pl.ds(c * kc, kc), pl.ds(col, tn)], stage_ref.at[slot], sem.at[slot])

    @pl.when((n == 0) & (m == 0))
    def _():
        chunk_copy(0, 0, 0).start()
        for c in range(mb):
            if c + 1 < mb:
                chunk_copy(0, c + 1, (c + 1) % 2).start()
            chunk_copy(0, c, c % 2).wait()
            wb_ref[0, c * kc:(c + 1) * kc, :] = stage_ref[c % 2].astype(BF16)

    if nb > 1:
        @pl.when(n + 1 < nb)
        def _():
            chunk_copy(n + 1, m, m % 2).start()

        @pl.when((n + 1 < nb) & (m > 0))
        def _():
            chunk_copy(n + 1, m - 1, (m - 1) % 2).wait()
            row = pl.multiple_of((m - 1) * kc, kc)
            wb_ref[1 - cur, pl.ds(row, kc), :] = stage_ref[(m - 1) % 2].astype(BF16)

        @pl.when((n > 0) & (m == 0))
        def _():
            chunk_copy(n, mb - 1, (mb - 1) % 2).wait()
            wb_ref[cur, (mb - 1) * kc:, :] = stage_ref[(mb - 1) % 2].astype(BF16)

    rows = a_ref.shape[0] // m_chunks
    for c in range(m_chunks):
        rs = pl.ds(c * rows, rows)
        lhs = a_ref[rs, :]
        if with_norm:
            ms = jnp.mean(lhs * lhs, axis=-1, keepdims=True)
            lhs = (lhs * lax.rsqrt(ms + NORM_EPS) * g_ref[...]).astype(BF16)
            h_ref[rs, :] = lhs
        acc = jnp.dot(lhs, wb_ref[cur], preferred_element_type=F32)
        epilogue(acc, o_ref.at[rs, :], *[e.at[rs, :] for e in extras])


def _matmul(name, a, w, epilogue, col_ranges, width, tm, tn, lane_tiles=(), tiles=(), m_chunks=1,
            norm_gain=None):
    t, k = a.shape
    per_range = width // tn
    nb, mb = per_range * len(col_ranges), t // tm
    starts = [c // tn for c in col_ranges]
    with_norm = norm_gain is not None
    assert not with_norm or nb == 1

    def w_block(n):
        blk = n % per_range + starts[0]
        for r in range(1, len(starts)):
            blk = jnp.where(n // per_range == r, n % per_range + starts[r], blk)
        return blk

    extra_specs = [pl.BlockSpec((tm, LANES), lambda n, m: (m, 0)) for _ in lane_tiles]
    extra_specs += [pl.BlockSpec((tm, tn), functools.partial(lambda n, m, off: (m, n + off), off=off))
                    for _, off in tiles]
    extras = list(lane_tiles) + [arr for arr, _ in tiles]
    body = functools.partial(_matmul_kernel, epilogue=epilogue, n_extra=len(extras), w_block=w_block, nb=nb, mb=mb,
                             m_chunks=m_chunks, with_norm=with_norm)
    in_specs = [pl.BlockSpec((tm, k), lambda n, m: (m, 0)), pl.BlockSpec(memory_space=pl.ANY)]
    out_specs = [pl.BlockSpec((tm, tn), lambda n, m: (m, n))]
    out_shape = [jax.ShapeDtypeStruct((t, nb * tn), BF16)]
    operands = [a, w]
    if with_norm:
        in_specs.append(pl.BlockSpec((1, k), lambda n, m: (0, 0)))
        operands.append(norm_gain)
        out_specs.append(pl.BlockSpec((tm, k), lambda n, m: (m, 0)))
        out_shape.append(jax.ShapeDtypeStruct((t, k), BF16))
    outs = pl.pallas_call(
        body,
        grid=(nb, mb),
        in_specs=in_specs + extra_specs,
        out_specs=out_specs,
        out_shape=out_shape,
        scratch_shapes=[pltpu.VMEM((min(nb, 2), k, tn), BF16), pltpu.VMEM((2, k // mb, tn), F32),
                        pltpu.SemaphoreType.DMA((2,))],
        compiler_params=_params("arbitrary", "arbitrary"),
        name=name,
    )(*operands, *extras)
    return outs if with_norm else outs[0]


def _attn_body(sink_ref, q_ref, kvp_ref, kvc_ref, ga_ref, o_ref, mprev_ref, mcur_ref, has_prev):
    pairs = Q_PER_KV // 2
    width = pairs * LANES
    key = lax.broadcasted_iota(jnp.int32, (BLOCK, width), 0)
    qry = lax.broadcasted_iota(jnp.int32, (BLOCK, width), 1) % LANES
    from_prev = key > qry
    left = lax.broadcasted_iota(jnp.int32, (BLOCK, LANES), 1) < HEAD_DIM
    lane_pair = lax.broadcasted_iota(jnp.int32, (1, width), 1) // LANES
    n_keys = (2 if has_prev else 1) * BLOCK
    zeros_t = jnp.zeros((HEAD_DIM, n_keys), F32)
    nt = (((1,), (1,)), ((), ()))
    log2e = 1.4426950408889634

    def head_halves(slab, e):
        swap = pltpu.roll(slab, HEAD_DIM, axis=1)
        zero = jnp.zeros_like(slab)
        if e == 0:
            return jnp.where(left, slab, zero).astype(BF16), jnp.where(left, zero, swap).astype(BF16)
        return jnp.where(left, swap, zero).astype(BF16), jnp.where(left, zero, slab).astype(BF16)

    def merged_max(s_prev, s_cur, sink):
        s = jnp.where(from_prev, s_prev if has_prev else -jnp.inf, s_cur)
        return s, jnp.maximum(jnp.max(s, axis=0, keepdims=True), sink)

    def probs(s, m, sink):
        p = jnp.exp2(s - m)
        denom = jnp.sum(p, axis=0, keepdims=True) + jnp.exp2(sink - m)
        p = p.astype(BF16)
        parts = ([p * mprev_ref[...]] if has_prev else []) + [p * mcur_ref[...]]
        return parts, 1.0 / denom

    v_t = {}

    def scores(kv_head):
        j, e = divmod(kv_head, 2)
        kcol, vcol = j * LANES, KV_WIDTH + j * LANES
        if e == 0:
            v_rows = ([kvp_ref[:, vcol:vcol + LANES]] if has_prev else []) + [kvc_ref[:, vcol:vcol + LANES]]
            v_t[j] = jnp.concatenate(v_rows, axis=0).astype(F32).T
        kc_l, kc_r = head_halves(kvc_ref[:, kcol:kcol + LANES].astype(F32), e)
        if has_prev:
            kp_l, kp_r = head_halves(kvp_ref[:, kcol:kcol + LANES].astype(F32), e)
            k_rows = jnp.concatenate([kp_l, kc_l, kp_r, kc_r], axis=0)
        else:
            k_rows = jnp.concatenate([kc_l, kc_r], axis=0)
        slabs = [slice((kv_head * pairs + pp) * LANES, (kv_head * pairs + pp + 1) * LANES) for pp in range(pairs)]
        q_rows = jnp.concatenate([q_ref[:, sl] for sl in slabs], axis=0)
        sink_l = jnp.zeros((1, width), F32)
        sink_r = jnp.zeros((1, width), F32)
        for pp in range(pairs):
            head = 2 * (kv_head * pairs + pp)
            sink_l = jnp.where(lane_pair == pp, sink_ref[head] * log2e, sink_l)
            sink_r = jnp.where(lane_pair == pp, sink_ref[head + 1] * log2e, sink_r)
        s = lax.dot_general(k_rows, q_rows, nt, preferred_element_type=F32)
        tiles = [s[i * BLOCK:(i + 1) * BLOCK] for i in range(s.shape[0] // BLOCK)]
        if has_prev:
            s_l, m_l = merged_max(tiles[0], tiles[1], sink_l)
            s_r, m_r = merged_max(tiles[2], tiles[3], sink_r)
        else:
            s_l, m_l = merged_max(None, tiles[0], sink_l)
            s_r, m_r = merged_max(None, tiles[1], sink_r)
        return dict(vt=v_t[j][e * HEAD_DIM:(e + 1) * HEAD_DIM], slabs=slabs, s_l=s_l, s_r=s_r, m_l=m_l, m_r=m_r,
                    sink_l=sink_l, sink_r=sink_r)

    def finish(st, after):
        vt = st["vt"]
        if after is not None:
            bits = lax.bitcast_convert_type(after[0], jnp.uint32) | lax.bitcast_convert_type(after[1], jnp.uint32)
            zero = lax.shift_right_logical(lax.shift_right_logical(bits, jnp.uint32(16)), jnp.uint32(16))
            vt = lax.bitcast_convert_type(lax.bitcast_convert_type(vt, jnp.uint32) + zero[:, :n_keys], F32)
        v_bd = jnp.concatenate([jnp.concatenate([vt, zeros_t], axis=1),
                                jnp.concatenate([zeros_t, vt], axis=1)], axis=0).astype(BF16)
        p_l, r_l = probs(st["s_l"], st["m_l"], st["sink_l"])
        p_r, r_r = probs(st["s_r"], st["m_r"], st["sink_r"])
        p2 = jnp.concatenate(p_l + p_r, axis=0)
        o_t = jnp.dot(v_bd, p2, preferred_element_type=F32)
        o_t = jnp.concatenate([o_t[:HEAD_DIM] * r_l, o_t[HEAD_DIM:] * r_r], axis=0)
        for pp, sl in enumerate(st["slabs"]):
            o = o_t[:, pp * LANES:(pp + 1) * LANES].T
            o_ref[:, sl] = (o * ga_ref[:, sl].astype(F32)).astype(o_ref.dtype)

    st = scores(0)
    for kv_head in range(N_KV_HEADS):
        nxt = scores(kv_head + 1) if kv_head + 1 < N_KV_HEADS else None
        finish(st, (nxt["m_l"], nxt["m_r"]) if nxt is not None else None)
        st = nxt


def _attn_kernel(sink_ref, q_ref, kvp_ref, kvc_ref, ga_ref, o_ref, mprev_ref, mcur_ref):
    assert WINDOW == BLOCK
    blk = pl.program_id(0) % (SEQ // BLOCK)
    width = mprev_ref.shape[1]
    key = lax.broadcasted_iota(jnp.int32, (BLOCK, width), 0)
    qry = lax.broadcasted_iota(jnp.int32, (BLOCK, width), 1) % LANES
    mprev_ref[...] = (key > qry).astype(BF16)
    mcur_ref[...] = (key <= qry).astype(BF16)
    args = (sink_ref, q_ref, kvp_ref, kvc_ref, ga_ref, o_ref, mprev_ref, mcur_ref)
    pl.when(blk > 0)(lambda: _attn_body(*args, has_prev=True))
    pl.when(blk == 0)(lambda: _attn_body(*args, has_prev=False))


def _attention(q, kv, act_a, sink):
    t = q.shape[0]
    nb = t // BLOCK
    return pl.pallas_call(
        _attn_kernel,
        grid_spec=pltpu.PrefetchScalarGridSpec(
            num_scalar_prefetch=1,
            grid=(nb,),
            in_specs=[pl.BlockSpec((BLOCK, ATTN_WIDTH), lambda i, s: (i, 0)),
                      pl.BlockSpec((BLOCK, 2 * KV_WIDTH), lambda i, s: (jnp.maximum(i - 1, 0), 0)),
                      pl.BlockSpec((BLOCK, 2 * KV_WIDTH), lambda i, s: (i, 0)),
                      pl.BlockSpec((BLOCK, ATTN_WIDTH), lambda i, s: (i, 0))],
            out_specs=pl.BlockSpec((BLOCK, ATTN_WIDTH), lambda i, s: (i, 0)),
            scratch_shapes=[pltpu.VMEM((BLOCK, (Q_PER_KV // 2) * LANES), BF16)] * 2,
        ),
        out_shape=jax.ShapeDtypeStruct((t, ATTN_WIDTH), BF16),
        compiler_params=_params("arbitrary"),
        name="swa_sink_attention",
    )(sink, q, kv, kv, act_a)


def _gating_kernel(u_ref, v_ref, gb_ref, lng_ref, lnb_ref, ws_ref, bt_ref, o_ref):
    v = v_ref[...].astype(F32)
    mu = jnp.mean(v, axis=-1, keepdims=True)
    vc = v - mu
    var = jnp.mean(vc * vc, axis=-1, keepdims=True)
    vn = (vc * lax.rsqrt(var + LN_EPS) * lng_ref[...] + lnb_ref[...]).astype(BF16)
    ti = lax.broadcasted_iota(jnp.int32, (GMLP_CHUNK, GMLP_CHUNK), 0)
    si = lax.broadcasted_iota(jnp.int32, (GMLP_CHUNK, GMLP_CHUNK), 1)
    causal = si <= ti
    bt = bt_ref[...]
    for g in range(GMLP_GROUPS):
        w = jnp.where(causal, ws_ref[g], 0.0).astype(BF16)
        bias = bt[:, g:g + 1]
        cols = slice(g * GMLP_GROUP_DIM, (g + 1) * GMLP_GROUP_DIM)
        for c in range(v.shape[0] // GMLP_CHUNK):
            rows = slice(c * GMLP_CHUNK, (c + 1) * GMLP_CHUNK)
            mixed = jnp.dot(w, vn[rows, cols], preferred_element_type=F32) + bias
            sg = u_ref[rows, cols].astype(F32) * mixed
            o_ref[rows, cols] = (sg * gb_ref[rows, cols].astype(F32)).astype(o_ref.dtype)


def _spatial_gating(act_uv, act_gate, ln_g, ln_b, w_s, b_t):
    t = act_uv.shape[0]
    r = 2 * GMLP_CHUNK
    w = GMLP_WIDTH
    return pl.pallas_call(
        _gating_kernel,
        grid=(t // r,),
        in_specs=[pl.BlockSpec((r, w), lambda i: (i, 0)),
                  pl.BlockSpec((r, w), lambda i: (i, 1)),
                  pl.BlockSpec((r, w), lambda i: (i, 1)),
                  pl.BlockSpec((1, w), lambda i: (0, 0)),
                  pl.BlockSpec((1, w), lambda i: (0, 0)),
                  pl.BlockSpec((GMLP_GROUPS, GMLP_CHUNK, GMLP_CHUNK), lambda i: (0, 0, 0)),
                  pl.BlockSpec((GMLP_CHUNK, GMLP_GROUPS), lambda i: (0, 0))],
        out_specs=pl.BlockSpec((r, w), lambda i: (i, 0)),
        out_shape=jax.ShapeDtypeStruct((t, w), BF16),
        compiler_params=_params("arbitrary"),
        name="spatial_gating",
    )(act_uv, act_uv, act_gate, ln_g, ln_b, w_s, b_t)


def _out_kernel(a_ref, w_ref, x_ref, g_ref, o_ref, y0_ref, y1_ref, ss0_ref, ss1_ref):
    i, j = pl.program_id(0), pl.program_id(1)
    col = pl.multiple_of(j * TN, TN)

    @pl.when((i == 0) & (j == 0))
    def _():
        y1_ref[...] = jnp.zeros_like(y1_ref)
        ss1_ref[...] = jnp.zeros_like(ss1_ref)

    def step(y_ref, ss_ref, yp_ref, ssp_ref):
        ms_prev = jnp.sum(ssp_ref[...], axis=-1, keepdims=True) * (1.0 / D_MODEL)
        o_ref[...] = yp_ref[:, pl.ds(col, TN)] * lax.rsqrt(ms_prev + NORM_EPS) * g_ref[:, pl.ds(col, TN)]
        y = x_ref[...] + jnp.dot(a_ref[...], w_ref[...], preferred_element_type=F32)
        sq = y * y
        part = sq[:, :LANES]
        for s in range(1, TN // LANES):
            part = part + sq[:, s * LANES:(s + 1) * LANES]
        ss_ref[...] = jnp.where(j == 0, part, ss_ref[...] + part)
        y_ref[:, pl.ds(col, TN)] = y

    pl.when(i % 2 == 0)(lambda: step(y0_ref, ss0_ref, y1_ref, ss1_ref))
    pl.when(i % 2 == 1)(lambda: step(y1_ref, ss1_ref, y0_ref, ss0_ref))


def _out_proj(a, w, x2, g_row):
    t, k = a.shape
    n = w.shape[1]
    last = t // TM_OUT - 1
    return pl.pallas_call(
        _out_kernel,
        grid=(t // TM_OUT + 1, n // TN),
        in_specs=[pl.BlockSpec((TM_OUT, k), lambda i, j: (jnp.minimum(i, last), 0)),
                  pl.BlockSpec((k, TN), lambda i, j: (0, j)),
                  pl.BlockSpec((TM_OUT, TN), lambda i, j: (jnp.minimum(i, last), j)),
                  pl.BlockSpec((1, n), lambda i, j: (0, 0))],
        out_specs=pl.BlockSpec((TM_OUT, TN), lambda i, j: (jnp.maximum(i - 1, 0), jnp.where(i == 0, 0, j))),
        out_shape=jax.ShapeDtypeStruct((t, n), F32),
        scratch_shapes=[pltpu.VMEM((TM_OUT, n), F32), pltpu.VMEM((TM_OUT, n), F32),
                        pltpu.VMEM((TM_OUT, LANES), F32), pltpu.VMEM((TM_OUT, LANES), F32)],
        compiler_params=_params("arbitrary", "arbitrary"),
        name="out_proj_residual_norm",
    )(a, w, x2, g_row)


def _rope_inv_freq_lanes():
    inv_freq = ROPE_THETA ** (-jnp.arange(ROPE_HALF, dtype=F32) * 2.0 / ROPE_DIM)
    j = jnp.arange(LANES) % HEAD_DIM
    return jnp.where(j < ROPE_DIM, inv_freq[j % ROPE_HALF], 0.0).astype(F32)[None, :]


def kernel(x, positions, norm_g, w_in, attn_sink, gmlp_ln_g, gmlp_ln_b, w_spatial, b_spatial, w_up_attn,
           w_up_gmlp, w_out, final_norm_g):
    b, s, d = x.shape
    t = b * s
    assert s == SEQ and d == D_MODEL and norm_g.shape[0] == DEPTH == 1
    x2 = x.reshape(t, d)
    cos_t, sin_t = _rope_tables(positions.reshape(t, 1), _rope_inv_freq_lanes())
    nw = D_MODEL // TN
    w_i = w_in[0]
    kv, h = _matmul("norm_in_proj_kv", x2, w_i, _epilogue_kv, [OFF_KV], 2 * KV_WIDTH, TM_NORM, TN,
                    lane_tiles=(cos_t, sin_t), m_chunks=NORM_CHUNKS, norm_gain=norm_g)
    q = _matmul("in_proj_q", h, w_i, _epilogue_q, [OFF_Q], ATTN_WIDTH, TM, TN, lane_tiles=(cos_t, sin_t),
                m_chunks=Q_CHUNKS)
    act_gate = _matmul("in_proj_silu", h, w_i, functools.partial(_epilogue_act, act=_silu),
                       [OFF_GATE_A, OFF_GATE_B], ATTN_WIDTH, TM, TN)
    act_uv = _matmul("in_proj_gelu", h, w_i, functools.partial(_epilogue_act, act=jax.nn.gelu),
                     [OFF_U], 2 * GMLP_WIDTH, TM, TN)
    act_merge = _matmul("in_proj_sigmoid", h, w_i, functools.partial(_epilogue_act, act=_sigmoid),
                        [OFF_MERGE], 2 * D_MODEL, TM, TN)
    attn_g = _attention(q, kv, act_gate, attn_sink[0])
    ya = _matmul("up_attn", attn_g, w_up_attn[0], _epilogue_up_attn, [0], D_MODEL, TM, TN, tiles=[(act_merge, 0)])
    sgb = _spatial_gating(act_uv, act_gate, gmlp_ln_g, gmlp_ln_b, w_spatial[0], b_spatial[0].T)
    merged = _matmul("up_gmlp_merge", sgb, w_up_gmlp[0], _epilogue_up_gmlp, [0], D_MODEL, TM, TN,
                     tiles=[(act_merge, nw), (ya, 0)])
    out = _out_proj(merged, w_out[0].astype(BF16), x2, final_norm_g[None, :])
    return out.reshape(b, s, d)
```

```python
import functools

import jax
import jax.numpy as jnp
from jax import lax
from jax.experimental import pallas as pl
from jax.experimental.pallas import tpu as pltpu

D_MODEL = 4096
SEQ = 4096
DEPTH = 1
N_Q_HEADS = 64
N_KV_HEADS = 8
HEAD_DIM = 64
Q_PER_KV = N_Q_HEADS // N_KV_HEADS
ATTN_WIDTH = N_Q_HEADS * HEAD_DIM
KV_WIDTH = N_KV_HEADS * HEAD_DIM
WINDOW = 128
BLOCK = 128
ROPE_THETA = 500000.0
ROPE_DIM = HEAD_DIM // 4
ROPE_HALF = ROPE_DIM // 2
GMLP_WIDTH = D_MODEL
GMLP_GROUPS = 8
GMLP_GROUP_DIM = GMLP_WIDTH // GMLP_GROUPS
GMLP_CHUNK = 128
NORM_EPS = 1e-5
LN_EPS = 1e-5

OFF_Q = 0
OFF_KV = ATTN_WIDTH
OFF_GATE_A = OFF_KV + 2 * KV_WIDTH
OFF_U = OFF_GATE_A + ATTN_WIDTH
OFF_GATE_B = OFF_U + 2 * GMLP_WIDTH
OFF_MERGE = OFF_GATE_B + GMLP_WIDTH

LANES = 128
BF16_SUBLANES = 16
VMEM_LIMIT = 56 * 1024 * 1024

TM = 1024
TN = 1024
Q_CHUNKS = 4
TM_NORM = 512
NORM_CHUNKS = 2
TM_OUT = 512

BF16 = jnp.bfloat16
F32 = jnp.float32


def _params(*sem):
    return pltpu.CompilerParams(dimension_semantics=sem, vmem_limit_bytes=VMEM_LIMIT)


def _rope_table_kernel(pos_ref, invf_ref, cos_ref, sin_ref):
    pad = jnp.zeros((HEAD_DIM - ROPE_DIM, LANES), F32)
    for g in range(pos_ref.shape[0]):
        ang = pos_ref[g:g + 1, :].astype(F32) * invf_ref[...]
        for fn, out_ref in ((jnp.cos, cos_ref), (jnp.sin, sin_ref)):
            val = fn(ang)
            head = jnp.concatenate([val, val, pad], axis=0)
            out_ref[g * LANES:(g + 1) * LANES, :] = jnp.concatenate([head, head], axis=0).T


def _rope_tables(pos_rows, invf_rows):
    groups = 8
    n = pos_rows.shape[0]
    return pl.pallas_call(
        _rope_table_kernel,
        grid=(n // groups,),
        in_specs=[pl.BlockSpec((groups, LANES), lambda i: (i, 0)),
                  pl.BlockSpec((ROPE_HALF, LANES), lambda i: (0, 0))],
        out_specs=[pl.BlockSpec((groups * LANES, LANES), lambda i: (i, 0)),
                   pl.BlockSpec((groups * LANES, LANES), lambda i: (i, 0))],
        out_shape=[jax.ShapeDtypeStruct((n * LANES, LANES), F32)] * 2,
        compiler_params=_params("arbitrary"),
        name="rope_tables",
    )(pos_rows, invf_rows)


def _rope_slab(a, cos, sin, lo, hi):
    up = pltpu.roll(a, LANES - ROPE_HALF, axis=1)
    dn = pltpu.roll(a, ROPE_HALF, axis=1)
    return jnp.where(lo, a * cos - up * sin, jnp.where(hi, a * cos + dn * sin, a))


def _rope_masks(rows):
    j = lax.broadcasted_iota(jnp.int32, (rows, LANES), 1) & (HEAD_DIM - 1)
    return j < ROPE_HALF, (j >= ROPE_HALF) & (j < ROPE_DIM)


def _epilogue_q(acc, o_ref, cos_ref, sin_ref):
    cos, sin = cos_ref[...], sin_ref[...]
    lo, hi = _rope_masks(acc.shape[0])
    scale = HEAD_DIM ** -0.5 * 1.4426950408889634
    for s in range(acc.shape[1] // LANES):
        sl = slice(s * LANES, (s + 1) * LANES)
        o_ref[:, sl] = (_rope_slab(acc[:, sl], cos, sin, lo, hi) * scale).astype(o_ref.dtype)


def _epilogue_kv(acc, o_ref, cos_ref, sin_ref):
    cos, sin = cos_ref[...], sin_ref[...]
    lo, hi = _rope_masks(acc.shape[0])
    for s in range(acc.shape[1] // LANES):
        sl = slice(s * LANES, (s + 1) * LANES)
        a = acc[:, sl]
        if s * LANES < KV_WIDTH:
            a = _rope_slab(a, cos, sin, lo, hi)
        o_ref[:, sl] = a.astype(o_ref.dtype)


def _sigmoid(x):
    return 0.5 * jnp.tanh(0.5 * x) + 0.5


def _silu(x):
    half = 0.5 * x
    return half * jnp.tanh(half) + half


def _epilogue_act(acc, o_ref, *, act):
    for s in range(acc.shape[1] // LANES):
        sl = slice(s * LANES, (s + 1) * LANES)
        o_ref[:, sl] = act(acc[:, sl]).astype(o_ref.dtype)


def _epilogue_up_attn(acc, o_ref, sa_ref):
    for s in range(acc.shape[1] // LANES):
        sl = slice(s * LANES, (s + 1) * LANES)
        o_ref[:, sl] = (sa_ref[:, sl].astype(F32) * acc[:, sl]).astype(o_ref.dtype)


def _epilogue_up_gmlp(acc, o_ref, sb_ref, ya_ref):
    for s in range(acc.shape[1] // LANES):
        sl = slice(s * LANES, (s + 1) * LANES)
        merged = ya_ref[:, sl].astype(F32) + sb_ref[:, sl].astype(F32) * acc[:, sl]
        o_ref[:, sl] = merged.astype(o_ref.dtype)


def _matmul_kernel(a_ref, w_hbm, *rest, epilogue, n_extra, w_block, nb, mb, m_chunks, with_norm, with_side):
    if with_norm:
        g_ref, rest = rest[0], rest[1:]
    extras, rest = rest[:n_extra], rest[n_extra:]
    if with_side:
        side_in_ref, rest = rest[0], rest[1:]
    o_ref, rest = rest[0], rest[1:]
    if with_norm:
        h_ref, rest = rest[0], rest[1:]
    if with_side:
        side_out_ref, rest = rest[0], rest[1:]
        side_out_ref[...] = side_in_ref[...].astype(BF16)
    wb_ref, stage_ref, sem = rest
    n, m = pl.program_id(0), pl.program_id(1)
    k, tn = wb_ref.shape[1:]
    kc = k // mb
    cur = n % 2

    def chunk_copy(block, c, slot):
        col = pl.multiple_of(w_block(block) * tn, tn)
        return pltpu.make_async_copy(w_hbm.at[pl.ds(c * kc, kc), pl.ds(col, tn)], stage_ref.at[slot], sem.at[slot])

    @pl.when((n == 0) & (m == 0))
    def _():
        chunk_copy(0, 0, 0).start()
        for c in range(mb):
            if c + 1 < mb:
                chunk_copy(0, c + 1, (c + 1) % 2).start()
            chunk_copy(0, c, c % 2).wait()
            wb_ref[0, c * kc:(c + 1) * kc, :] = stage_ref[c % 2].astype(BF16)

    if nb > 1:
        @pl.when(n + 1 < nb)
        def _():
            chunk_copy(n + 1, m, m % 2).start()

        @pl.when((n + 1 < nb) & (m > 0))
        def _():
            chunk_copy(n + 1, m - 1, (m - 1) % 2).wait()
            row = pl.multiple_of((m - 1) * kc, kc)
            wb_ref[1 - cur, pl.ds(row, kc), :] = stage_ref[(m - 1) % 2].astype(BF16)

        @pl.when((n > 0) & (m == 0))
        def _():
            chunk_copy(n, mb - 1, (mb - 1) % 2).wait()
            wb_ref[cur, (mb - 1) * kc:, :] = stage_ref[(mb - 1) % 2].astype(BF16)

    rows = a_ref.shape[0] // m_chunks
    for c in range(m_chunks):
        rs = pl.ds(c * rows, rows)
        lhs = a_ref[rs, :]
        if with_norm:
            ms = jnp.mean(lhs * lhs, axis=-1, keepdims=True)
            lhs = (lhs * lax.rsqrt(ms + NORM_EPS) * g_ref[...]).astype(BF16)
            h_ref[rs, :] = lhs
        acc = jnp.dot(lhs, wb_ref[cur], preferred_element_type=F32)
        epilogue(acc, o_ref.at[rs, :], *[e.at[rs, :] for e in extras])


def _matmul(name, a, w, epilogue, col_ranges, width, tm, tn, lane_tiles=(), tiles=(), m_chunks=1,
            norm_gain=None, side_convert=None):
    t, k = a.shape
    per_range = width // tn
    nb, mb = per_range * len(col_ranges), t // tm
    starts = [c // tn for c in col_ranges]
    with_norm = norm_gain is not None
    with_side = side_convert is not None
    assert not with_norm or nb == 1

    def w_block(n):
        blk = n % per_range + starts[0]
        for r in range(1, len(starts)):
            blk = jnp.where(n // per_range == r, n % per_range + starts[r], blk)
        return blk

    extra_specs = [pl.BlockSpec((tm, LANES), lambda n, m: (m, 0)) for _ in lane_tiles]
    extra_specs += [pl.BlockSpec((tm, tn), functools.partial(lambda n, m, off: (m, n + off), off=off))
                    for _, off in tiles]
    extras = list(lane_tiles) + [arr for arr, _ in tiles]
    body = functools.partial(_matmul_kernel, epilogue=epilogue, n_extra=len(extras), w_block=w_block, nb=nb, mb=mb,
                             m_chunks=m_chunks, with_norm=with_norm, with_side=with_side)
    in_specs = [pl.BlockSpec((tm, k), lambda n, m: (m, 0)), pl.BlockSpec(memory_space=pl.ANY)]
    out_specs = [pl.BlockSpec((tm, tn), lambda n, m: (m, n))]
    out_shape = [jax.ShapeDtypeStruct((t, nb * tn), BF16)]
    operands = [a, w]
    if with_norm:
        in_specs.append(pl.BlockSpec((1, k), lambda n, m: (0, 0)))
        operands.append(norm_gain)
        out_specs.append(pl.BlockSpec((tm, k), lambda n, m: (m, 0)))
        out_shape.append(jax.ShapeDtypeStruct((t, k), BF16))
    in_specs += extra_specs
    operands += extras
    if with_side:
        slab = (side_convert.shape[0] // (nb * mb), side_convert.shape[1])
        in_specs.append(pl.BlockSpec(slab, lambda n, m: (n * mb + m, 0)))
        operands.append(side_convert)
        out_specs.append(pl.BlockSpec(slab, lambda n, m: (n * mb + m, 0)))
        out_shape.append(jax.ShapeDtypeStruct(side_convert.shape, BF16))
    outs = pl.pallas_call(
        body,
        grid=(nb, mb),
        in_specs=in_specs,
        out_specs=out_specs,
        out_shape=out_shape,
        scratch_shapes=[pltpu.VMEM((min(nb, 2), k, tn), BF16), pltpu.VMEM((2, k // mb, tn), F32),
                        pltpu.SemaphoreType.DMA((2,))],
        compiler_params=_params("arbitrary", "arbitrary"),
        name=name,
    )(*operands)
    return outs if len(outs) > 1 else outs[0]


def _attn_body(sink_ref, q_ref, kvp_ref, kvc_ref, ga_ref, o_ref, mprev_ref, mcur_ref, has_prev):
    pairs = Q_PER_KV // 2
    width = pairs * LANES
    key = lax.broadcasted_iota(jnp.int32, (BLOCK, width), 0)
    qry = lax.broadcasted_iota(jnp.int32, (BLOCK, width), 1) % LANES
    from_prev = key > qry
    left = lax.broadcasted_iota(jnp.int32, (BLOCK, LANES), 1) < HEAD_DIM
    lane_pair = lax.broadcasted_iota(jnp.int32, (1, width), 1) // LANES
    n_keys = (2 if has_prev else 1) * BLOCK
    zeros_t = jnp.zeros((HEAD_DIM, n_keys), F32)
    nt = (((1,), (1,)), ((), ()))
    log2e = 1.4426950408889634

    def head_halves(slab, e):
        swap = pltpu.roll(slab, HEAD_DIM, axis=1)
        zero = jnp.zeros_like(slab)
        if e == 0:
            return jnp.where(left, slab, zero).astype(BF16), jnp.where(left, zero, swap).astype(BF16)
        return jnp.where(left, swap, zero).astype(BF16), jnp.where(left, zero, slab).astype(BF16)

    def merged_max(s_prev, s_cur, sink):
        s = jnp.where(from_prev, s_prev if has_prev else -jnp.inf, s_cur)
        return s, jnp.maximum(jnp.max(s, axis=0, keepdims=True), sink)

    def probs(s, m):
        p = jnp.exp2(s - m).astype(BF16)
        return ([p * mprev_ref[...]] if has_prev else []) + [p * mcur_ref[...]]

    sum_row = lax.broadcasted_iota(jnp.int32, (BF16_SUBLANES, 2 * n_keys), 0)
    sum_col = lax.broadcasted_iota(jnp.int32, (BF16_SUBLANES, 2 * n_keys), 1)
    sum_rows = jnp.where((sum_row == 0) == (sum_col < n_keys), 1.0, 0.0) * (sum_row < 2)

    v_t = {}

    def scores(kv_head):
        j, e = divmod(kv_head, 2)
        kcol, vcol = j * LANES, KV_WIDTH + j * LANES
        if e == 0:
            v_rows = ([kvp_ref[:, vcol:vcol + LANES]] if has_prev else []) + [kvc_ref[:, vcol:vcol + LANES]]
            v_t[j] = jnp.concatenate(v_rows, axis=0).astype(F32).T
        kc_l, kc_r = head_halves(kvc_ref[:, kcol:kcol + LANES].astype(F32), e)
        if has_prev:
            kp_l, kp_r = head_halves(kvp_ref[:, kcol:kcol + LANES].astype(F32), e)
            k_rows = jnp.concatenate([kp_l, kc_l, kp_r, kc_r], axis=0)
        else:
            k_rows = jnp.concatenate([kc_l, kc_r], axis=0)
        slabs = [slice((kv_head * pairs + pp) * LANES, (kv_head * pairs + pp + 1) * LANES) for pp in range(pairs)]
        q_rows = jnp.concatenate([q_ref[:, sl] for sl in slabs], axis=0)
        sink_l = jnp.zeros((1, width), F32)
        sink_r = jnp.zeros((1, width), F32)
        for pp in range(pairs):
            head = 2 * (kv_head * pairs + pp)
            sink_l = jnp.where(lane_pair == pp, sink_ref[head] * log2e, sink_l)
            sink_r = jnp.where(lane_pair == pp, sink_ref[head + 1] * log2e, sink_r)
        s = lax.dot_general(k_rows, q_rows, nt, preferred_element_type=F32)
        tiles = [s[i * BLOCK:(i + 1) * BLOCK] for i in range(s.shape[0] // BLOCK)]
        if has_prev:
            s_l, m_l = merged_max(tiles[0], tiles[1], sink_l)
            s_r, m_r = merged_max(tiles[2], tiles[3], sink_r)
        else:
            s_l, m_l = merged_max(None, tiles[0], sink_l)
            s_r, m_r = merged_max(None, tiles[1], sink_r)
        return dict(vt=v_t[j][e * HEAD_DIM:(e + 1) * HEAD_DIM], slabs=slabs, s_l=s_l, s_r=s_r, m_l=m_l, m_r=m_r,
                    sink_l=sink_l, sink_r=sink_r)

    def finish(st, after):
        vt = st["vt"]
        if after is not None:
            bits = lax.bitcast_convert_type(after[0], jnp.uint32) | lax.bitcast_convert_type(after[1], jnp.uint32)
            zero = lax.shift_right_logical(lax.shift_right_logical(bits, jnp.uint32(16)), jnp.uint32(16))
            vt = lax.bitcast_convert_type(lax.bitcast_convert_type(vt, jnp.uint32) + zero[:, :n_keys], F32)
        v_bd = jnp.concatenate([jnp.concatenate([vt, zeros_t], axis=1),
                                jnp.concatenate([zeros_t, vt], axis=1), sum_rows], axis=0).astype(BF16)
        p2 = jnp.concatenate(probs(st["s_l"], st["m_l"]) + probs(st["s_r"], st["m_r"]), axis=0)
        o_t = jnp.dot(v_bd, p2, preferred_element_type=F32)
        d_l = o_t[2 * HEAD_DIM:2 * HEAD_DIM + 1] + jnp.exp2(st["sink_l"] - st["m_l"])
        d_r = o_t[2 * HEAD_DIM + 1:2 * HEAD_DIM + 2] + jnp.exp2(st["sink_r"] - st["m_r"])
        o_t = jnp.concatenate([o_t[:HEAD_DIM] * (1.0 / d_l), o_t[HEAD_DIM:2 * HEAD_DIM] * (1.0 / d_r)], axis=0)
        for pp, sl in enumerate(st["slabs"]):
            o = o_t[:, pp * LANES:(pp + 1) * LANES].T
            o_ref[:, sl] = (o * ga_ref[:, sl].astype(F32)).astype(o_ref.dtype)

    st = scores(0)
    for kv_head in range(N_KV_HEADS):
        nxt = scores(kv_head + 1) if kv_head + 1 < N_KV_HEADS else None
        finish(st, (nxt["m_l"], nxt["m_r"]) if nxt is not None else None)
        st = nxt


def _attn_kernel(sink_ref, q_ref, kvp_ref, kvc_ref, ga_ref, o_ref, mprev_ref, mcur_ref):
    assert WINDOW == BLOCK
    blk = pl.program_id(0) % (SEQ // BLOCK)
    width = mprev_ref.shape[1]
    key = lax.broadcasted_iota(jnp.int32, (BLOCK, width), 0)
    qry = lax.broadcasted_iota(jnp.int32, (BLOCK, width), 1) % LANES
    mprev_ref[...] = (key > qry).astype(BF16)
    mcur_ref[...] = (key <= qry).astype(BF16)
    args = (sink_ref, q_ref, kvp_ref, kvc_ref, ga_ref, o_ref, mprev_ref, mcur_ref)
    pl.when(blk > 0)(lambda: _attn_body(*args, has_prev=True))
    pl.when(blk == 0)(lambda: _attn_body(*args, has_prev=False))


def _attention(q, kv, act_a, sink):
    t = q.shape[0]
    nb = t // BLOCK
    return pl.pallas_call(
        _attn_kernel,
        grid_spec=pltpu.PrefetchScalarGridSpec(
            num_scalar_prefetch=1,
            grid=(nb,),
            in_specs=[pl.BlockSpec((BLOCK, ATTN_WIDTH), lambda i, s: (i, 0)),
                      pl.BlockSpec((BLOCK, 2 * KV_WIDTH), lambda i, s: (jnp.maximum(i - 1, 0), 0)),
                      pl.BlockSpec((BLOCK, 2 * KV_WIDTH), lambda i, s: (i, 0)),
                      pl.BlockSpec((BLOCK, ATTN_WIDTH), lambda i, s: (i, 0))],
            out_specs=pl.BlockSpec((BLOCK, ATTN_WIDTH), lambda i, s: (i, 0)),
            scratch_shapes=[pltpu.VMEM((BLOCK, (Q_PER_KV // 2) * LANES), BF16)] * 2,
        ),
        out_shape=jax.ShapeDtypeStruct((t, ATTN_WIDTH), BF16),
        compiler_params=_params("arbitrary"),
        name="swa_sink_attention",
    )(sink, q, kv, kv, act_a)


def _gating_kernel(u_ref, v_ref, gb_ref, lng_ref, lnb_ref, ws_ref, bt_ref, o_ref):
    v = v_ref[...].astype(F32)
    mu = jnp.mean(v, axis=-1, keepdims=True)
    vc = v - mu
    var = jnp.mean(vc * vc, axis=-1, keepdims=True)
    vn = (vc * lax.rsqrt(var + LN_EPS) * lng_ref[...] + lnb_ref[...]).astype(BF16)
    ti = lax.broadcasted_iota(jnp.int32, (GMLP_CHUNK, GMLP_CHUNK), 0)
    si = lax.broadcasted_iota(jnp.int32, (GMLP_CHUNK, GMLP_CHUNK), 1)
    causal = si <= ti
    bt = bt_ref[...]
    for g in range(GMLP_GROUPS):
        w = jnp.where(causal, ws_ref[g], 0.0).astype(BF16)
        bias = bt[:, g:g + 1]
        cols = slice(g * GMLP_GROUP_DIM, (g + 1) * GMLP_GROUP_DIM)
        for c in range(v.shape[0] // GMLP_CHUNK):
            rows = slice(c * GMLP_CHUNK, (c + 1) * GMLP_CHUNK)
            mixed = jnp.dot(w, vn[rows, cols], preferred_element_type=F32) + bias
            sg = u_ref[rows, cols].astype(F32) * mixed
            o_ref[rows, cols] = (sg * gb_ref[rows, cols].astype(F32)).astype(o_ref.dtype)


def _spatial_gating(act_uv, act_gate, ln_g, ln_b, w_s, b_t):
    t = act_uv.shape[0]
    r = 2 * GMLP_CHUNK
    w = GMLP_WIDTH
    return pl.pallas_call(
        _gating_kernel,
        grid=(t // r,),
        in_specs=[pl.BlockSpec((r, w), lambda i: (i, 0)),
                  pl.BlockSpec((r, w), lambda i: (i, 1)),
                  pl.BlockSpec((r, w), lambda i: (i, 1)),
                  pl.BlockSpec((1, w), lambda i: (0, 0)),
                  pl.BlockSpec((1, w), lambda i: (0, 0)),
                  pl.BlockSpec((GMLP_GROUPS, GMLP_CHUNK, GMLP_CHUNK), lambda i: (0, 0, 0)),
                  pl.BlockSpec((GMLP_CHUNK, GMLP_GROUPS), lambda i: (0, 0))],
        out_specs=pl.BlockSpec((r, w), lambda i: (i, 0)),
        out_shape=jax.ShapeDtypeStruct((t, w), BF16),
        compiler_params=_params("arbitrary"),
        name="spatial_gating",
    )(act_uv, act_uv, act_gate, ln_g, ln_b, w_s, b_t)


def _out_kernel(a_ref, w_ref, x_ref, g_ref, o_ref, y0_ref, y1_ref, ss0_ref, ss1_ref):
    i, j = pl.program_id(0), pl.program_id(1)
    col = pl.multiple_of(j * TN, TN)

    @pl.when((i == 0) & (j == 0))
    def _():
        y1_ref[...] = jnp.zeros_like(y1_ref)
        ss1_ref[...] = jnp.zeros_like(ss1_ref)

    def step(y_ref, ss_ref, yp_ref, ssp_ref):
        ms_prev = jnp.sum(ssp_ref[...], axis=-1, keepdims=True) * (1.0 / D_MODEL)
        o_ref[...] = yp_ref[:, pl.ds(col, TN)] * lax.rsqrt(ms_prev + NORM_EPS) * g_ref[:, pl.ds(col, TN)]
        y = x_ref[...] + jnp.dot(a_ref[...], w_ref[...], preferred_element_type=F32)
        sq = y * y
        part = sq[:, :LANES]
        for s in range(1, TN // LANES):
            part = part + sq[:, s * LANES:(s + 1) * LANES]
        ss_ref[...] = jnp.where(j == 0, part, ss_ref[...] + part)
        y_ref[:, pl.ds(col, TN)] = y

    pl.when(i % 2 == 0)(lambda: step(y0_ref, ss0_ref, y1_ref, ss1_ref))
    pl.when(i % 2 == 1)(lambda: step(y1_ref, ss1_ref, y0_ref, ss0_ref))


def _out_proj(a, w, x2, g_row):
    t, k = a.shape
    n = w.shape[1]
    last = t // TM_OUT - 1
    return pl.pallas_call(
        _out_kernel,
        grid=(t // TM_OUT + 1, n // TN),
        in_specs=[pl.BlockSpec((TM_OUT, k), lambda i, j: (jnp.minimum(i, last), 0)),
                  pl.BlockSpec((k, TN), lambda i, j: (0, j)),
                  pl.BlockSpec((TM_OUT, TN), lambda i, j: (jnp.minimum(i, last), j)),
                  pl.BlockSpec((1, n), lambda i, j: (0, 0))],
        out_specs=pl.BlockSpec((TM_OUT, TN), lambda i, j: (jnp.maximum(i - 1, 0), jnp.where(i == 0, 0, j))),
        out_shape=jax.ShapeDtypeStruct((t, n), F32),
        scratch_shapes=[pltpu.VMEM((TM_OUT, n), F32), pltpu.VMEM((TM_OUT, n), F32),
                        pltpu.VMEM((TM_OUT, LANES), F32), pltpu.VMEM((TM_OUT, LANES), F32)],
        compiler_params=_params("arbitrary", "arbitrary"),
        name="out_proj_residual_norm",
    )(a, w, x2, g_row)


def _rope_inv_freq_rows():
    inv_freq = ROPE_THETA ** (-jnp.arange(ROPE_HALF, dtype=F32) * 2.0 / ROPE_DIM)
    return jnp.broadcast_to(inv_freq[:, None], (ROPE_HALF, LANES))


def kernel(x, positions, norm_g, w_in, attn_sink, gmlp_ln_g, gmlp_ln_b, w_spatial, b_spatial, w_up_attn,
           w_up_gmlp, w_out, final_norm_g):
    b, s, d = x.shape
    t = b * s
    assert s == SEQ and d == D_MODEL and norm_g.shape[0] == DEPTH == 1
    x2 = x.reshape(t, d)
    cos_t, sin_t = _rope_tables(positions.reshape(t // LANES, LANES), _rope_inv_freq_rows())
    nw = D_MODEL // TN
    w_i = w_in[0]
    kv, h = _matmul("norm_in_proj_kv", x2, w_i, _epilogue_kv, [OFF_KV], 2 * KV_WIDTH, TM_NORM, TN,
                    lane_tiles=(cos_t, sin_t), m_chunks=NORM_CHUNKS, norm_gain=norm_g)
    q = _matmul("in_proj_q", h, w_i, _epilogue_q, [OFF_Q], ATTN_WIDTH, TM, TN, lane_tiles=(cos_t, sin_t),
                m_chunks=Q_CHUNKS)
    act_gate = _matmul("in_proj_silu", h, w_i, functools.partial(_epilogue_act, act=_silu),
                       [OFF_GATE_A, OFF_GATE_B], ATTN_WIDTH, TM, TN)
    act_uv = _matmul("in_proj_gelu", h, w_i, functools.partial(_epilogue_act, act=jax.nn.gelu),
                     [OFF_U], 2 * GMLP_WIDTH, TM, TN)
    act_merge = _matmul("in_proj_sigmoid", h, w_i, functools.partial(_epilogue_act, act=_sigmoid),
                        [OFF_MERGE], 2 * D_MODEL, TM, TN)
    attn_g = _attention(q, kv, act_gate, attn_sink[0])
    ya, w_out_b = _matmul("up_attn", attn_g, w_up_attn[0], _epilogue_up_attn, [0], D_MODEL, TM, TN,
                          tiles=[(act_merge, 0)], side_convert=w_out[0])
    sgb = _spatial_gating(act_uv, act_gate, gmlp_ln_g, gmlp_ln_b, w_spatial[0], b_spatial[0].T)
    merged = _matmul("up_gmlp_merge", sgb, w_up_gmlp[0], _epilogue_up_gmlp, [0], D_MODEL, TM, TN,
                     tiles=[(act_merge, nw), (ya, 0)])
    out = _out_proj(merged, w_out_b, x2, final_norm_g[None, :])
    return out.reshape(b, s, d)
```

```python
import functools

import jax
import jax.numpy as jnp
from jax import lax
from jax.experimental import pallas as pl
from jax.experimental.pallas import tpu as pltpu

D_MODEL = 4096
SEQ = 4096
DEPTH = 1
N_Q_HEADS = 64
N_KV_HEADS = 8
HEAD_DIM = 64
Q_PER_KV = N_Q_HEADS // N_KV_HEADS
ATTN_WIDTH = N_Q_HEADS * HEAD_DIM
KV_WIDTH = N_KV_HEADS * HEAD_DIM
WINDOW = 128
BLOCK = 128
ROPE_THETA = 500000.0
ROPE_DIM = HEAD_DIM // 4
ROPE_HALF = ROPE_DIM // 2
GMLP_WIDTH = D_MODEL
GMLP_GROUPS = 8
GMLP_GROUP_DIM = GMLP_WIDTH // GMLP_GROUPS
GMLP_CHUNK = 128
NORM_EPS = 1e-5
LN_EPS = 1e-5

OFF_Q = 0
OFF_KV = ATTN_WIDTH
OFF_GATE_A = OFF_KV + 2 * KV_WIDTH
OFF_U = OFF_GATE_A + ATTN_WIDTH
OFF_GATE_B = OFF_U + 2 * GMLP_WIDTH
OFF_MERGE = OFF_GATE_B + GMLP_WIDTH

LANES = 128
BF16_SUBLANES = 16
VMEM_LIMIT = 56 * 1024 * 1024

TM = 1024
TN = 1024
Q_CHUNKS = 4
TM_NORM = 512
Q_TAIL = 1024
NORM_CHUNKS = 2
TM_OUT = 512

BF16 = jnp.bfloat16
F32 = jnp.float32


def _params(*sem):
    return pltpu.CompilerParams(dimension_semantics=sem, vmem_limit_bytes=VMEM_LIMIT)


def _rope_table_kernel(pos_ref, invf_ref, cos_ref, sin_ref):
    pad = jnp.zeros((HEAD_DIM - ROPE_DIM, LANES), F32)
    for g in range(pos_ref.shape[0]):
        ang = pos_ref[g:g + 1, :].astype(F32) * invf_ref[...]
        for fn, out_ref in ((jnp.cos, cos_ref), (jnp.sin, sin_ref)):
            val = fn(ang)
            head = jnp.concatenate([val, val, pad], axis=0)
            out_ref[g * LANES:(g + 1) * LANES, :] = jnp.concatenate([head, head], axis=0).T


def _rope_tables(pos_rows, invf_rows):
    groups = 8
    n = pos_rows.shape[0]
    return pl.pallas_call(
        _rope_table_kernel,
        grid=(n // groups,),
        in_specs=[pl.BlockSpec((groups, LANES), lambda i: (i, 0)),
                  pl.BlockSpec((ROPE_HALF, LANES), lambda i: (0, 0))],
        out_specs=[pl.BlockSpec((groups * LANES, LANES), lambda i: (i, 0)),
                   pl.BlockSpec((groups * LANES, LANES), lambda i: (i, 0))],
        out_shape=[jax.ShapeDtypeStruct((n * LANES, LANES), F32)] * 2,
        compiler_params=_params("arbitrary"),
        name="rope_tables",
    )(pos_rows, invf_rows)


def _rope_slab(a, cos, sin, lo, hi):
    up = pltpu.roll(a, LANES - ROPE_HALF, axis=1)
    dn = pltpu.roll(a, ROPE_HALF, axis=1)
    return jnp.where(lo, a * cos - up * sin, jnp.where(hi, a * cos + dn * sin, a))


def _rope_masks(rows):
    j = lax.broadcasted_iota(jnp.int32, (rows, LANES), 1) & (HEAD_DIM - 1)
    return j < ROPE_HALF, (j >= ROPE_HALF) & (j < ROPE_DIM)


def _epilogue_q(acc, o_ref, cos_ref, sin_ref):
    cos, sin = cos_ref[...], sin_ref[...]
    lo, hi = _rope_masks(acc.shape[0])
    scale = HEAD_DIM ** -0.5 * 1.4426950408889634
    for s in range(acc.shape[1] // LANES):
        sl = slice(s * LANES, (s + 1) * LANES)
        o_ref[:, sl] = (_rope_slab(acc[:, sl], cos, sin, lo, hi) * scale).astype(o_ref.dtype)


def _epilogue_qkv(acc, o_ref, cos_ref, sin_ref):
    cos, sin = cos_ref[...], sin_ref[...]
    lo, hi = _rope_masks(acc.shape[0])
    scale = HEAD_DIM ** -0.5 * 1.4426950408889634
    for s in range(acc.shape[1] // LANES):
        sl = slice(s * LANES, (s + 1) * LANES)
        a = acc[:, sl]
        if s * LANES < Q_TAIL:
            a = _rope_slab(a, cos, sin, lo, hi) * scale
        elif s * LANES < Q_TAIL + KV_WIDTH:
            a = _rope_slab(a, cos, sin, lo, hi)
        o_ref[:, sl] = a.astype(o_ref.dtype)


def _sigmoid(x):
    return 0.5 * jnp.tanh(0.5 * x) + 0.5


def _silu(x):
    half = 0.5 * x
    return half * jnp.tanh(half) + half


def _epilogue_act(acc, o_ref, *, act):
    for s in range(acc.shape[1] // LANES):
        sl = slice(s * LANES, (s + 1) * LANES)
        o_ref[:, sl] = act(acc[:, sl]).astype(o_ref.dtype)


def _epilogue_up_attn(acc, o_ref, sa_ref):
    for s in range(acc.shape[1] // LANES):
        sl = slice(s * LANES, (s + 1) * LANES)
        o_ref[:, sl] = (sa_ref[:, sl].astype(F32) * acc[:, sl]).astype(o_ref.dtype)


def _epilogue_up_gmlp(acc, o_ref, sb_ref, ya_ref):
    for s in range(acc.shape[1] // LANES):
        sl = slice(s * LANES, (s + 1) * LANES)
        merged = ya_ref[:, sl].astype(F32) + sb_ref[:, sl].astype(F32) * acc[:, sl]
        o_ref[:, sl] = merged.astype(o_ref.dtype)


def _matmul_kernel(a_ref, w_hbm, *rest, epilogue, n_extra, w_col, nb, mb, m_chunks, with_norm, with_side):
    if with_norm:
        g_ref, rest = rest[0], rest[1:]
    extras, rest = rest[:n_extra], rest[n_extra:]
    if with_side:
        side_in_ref, rest = rest[0], rest[1:]
    o_ref, rest = rest[0], rest[1:]
    if with_norm:
        h_ref, rest = rest[0], rest[1:]
    if with_side:
        side_out_ref, rest = rest[0], rest[1:]
        side_out_ref[...] = side_in_ref[...].astype(BF16)
    wb_ref, stage_ref, sem = rest
    n, m = pl.program_id(0), pl.program_id(1)
    k, tn = wb_ref.shape[1:]
    kc = k // mb
    cur = n % 2

    def chunk_copy(block, c, slot):
        col = pl.multiple_of(w_col(block), LANES)
        return pltpu.make_async_copy(w_hbm.at[pl.ds(c * kc, kc), pl.ds(col, tn)], stage_ref.at[slot], sem.at[slot])

    @pl.when((n == 0) & (m == 0))
    def _():
        chunk_copy(0, 0, 0).start()
        for c in range(mb):
            if c + 1 < mb:
                chunk_copy(0, c + 1, (c + 1) % 2).start()
            chunk_copy(0, c, c % 2).wait()
            wb_ref[0, c * kc:(c + 1) * kc, :] = stage_ref[c % 2].astype(BF16)

    if nb > 1:
        @pl.when(n + 1 < nb)
        def _():
            chunk_copy(n + 1, m, m % 2).start()

        @pl.when((n + 1 < nb) & (m > 0))
        def _():
            chunk_copy(n + 1, m - 1, (m - 1) % 2).wait()
            row = pl.multiple_of((m - 1) * kc, kc)
            wb_ref[1 - cur, pl.ds(row, kc), :] = stage_ref[(m - 1) % 2].astype(BF16)

        @pl.when((n > 0) & (m == 0))
        def _():
            chunk_copy(n, mb - 1, (mb - 1) % 2).wait()
            wb_ref[cur, (mb - 1) * kc:, :] = stage_ref[(mb - 1) % 2].astype(BF16)

    rows = a_ref.shape[0] // m_chunks
    for c in range(m_chunks):
        rs = pl.ds(c * rows, rows)
        lhs = a_ref[rs, :]
        if with_norm:
            ms = jnp.mean(lhs * lhs, axis=-1, keepdims=True)
            lhs = (lhs * lax.rsqrt(ms + NORM_EPS) * g_ref[...]).astype(BF16)
            h_ref[rs, :] = lhs
        acc = jnp.dot(lhs, wb_ref[cur], preferred_element_type=F32)
        epilogue(acc, o_ref.at[rs, :], *[e.at[rs, :] for e in extras])


def _matmul(name, a, w, epilogue, col_ranges, width, tm, tn, lane_tiles=(), tiles=(), m_chunks=1,
            norm_gain=None, side_convert=None):
    t, k = a.shape
    per_range = width // tn
    nb, mb = per_range * len(col_ranges), t // tm
    with_norm = norm_gain is not None
    with_side = side_convert is not None
    assert not with_norm or nb == 1

    def w_col(n):
        col = col_ranges[0]
        for r in range(1, len(col_ranges)):
            col = jnp.where(n // per_range == r, col_ranges[r], col)
        return col + n % per_range * tn

    extra_specs = [pl.BlockSpec((tm, LANES), lambda n, m: (m, 0)) for _ in lane_tiles]
    extra_specs += [pl.BlockSpec((tm, tn), functools.partial(lambda n, m, off: (m, n + off), off=off))
                    for _, off in tiles]
    extras = list(lane_tiles) + [arr for arr, _ in tiles]
    body = functools.partial(_matmul_kernel, epilogue=epilogue, n_extra=len(extras), w_col=w_col, nb=nb, mb=mb,
                             m_chunks=m_chunks, with_norm=with_norm, with_side=with_side)
    in_specs = [pl.BlockSpec((tm, k), lambda n, m: (m, 0)), pl.BlockSpec(memory_space=pl.ANY)]
    out_specs = [pl.BlockSpec((tm, tn), lambda n, m: (m, n))]
    out_shape = [jax.ShapeDtypeStruct((t, nb * tn), BF16)]
    operands = [a, w]
    if with_norm:
        in_specs.append(pl.BlockSpec((1, k), lambda n, m: (0, 0)))
        operands.append(norm_gain)
        out_specs.append(pl.BlockSpec((tm, k), lambda n, m: (m, 0)))
        out_shape.append(jax.ShapeDtypeStruct((t, k), BF16))
    in_specs += extra_specs
    operands += extras
    if with_side:
        slab = (side_convert.shape[0] // (nb * mb), side_convert.shape[1])
        in_specs.append(pl.BlockSpec(slab, lambda n, m: (n * mb + m, 0)))
        operands.append(side_convert)
        out_specs.append(pl.BlockSpec(slab, lambda n, m: (n * mb + m, 0)))
        out_shape.append(jax.ShapeDtypeStruct(side_convert.shape, BF16))
    outs = pl.pallas_call(
        body,
        grid=(nb, mb),
        in_specs=in_specs,
        out_specs=out_specs,
        out_shape=out_shape,
        scratch_shapes=[pltpu.VMEM((min(nb, 2), k, tn), BF16), pltpu.VMEM((2, k // mb, tn), F32),
                        pltpu.SemaphoreType.DMA((2,))],
        compiler_params=_params("arbitrary", "arbitrary"),
        name=name,
    )(*operands)
    return outs if len(outs) > 1 else outs[0]


def _attn_body(sink_ref, q_ref, qt_ref, kvp_ref, kvc_ref, ga_ref, o_ref, mprev_ref, mcur_ref, has_prev):
    pairs = Q_PER_KV // 2
    width = pairs * LANES
    key = lax.broadcasted_iota(jnp.int32, (BLOCK, width), 0)
    qry = lax.broadcasted_iota(jnp.int32, (BLOCK, width), 1) % LANES
    from_prev = key > qry
    left = lax.broadcasted_iota(jnp.int32, (BLOCK, LANES), 1) < HEAD_DIM
    lane_pair = lax.broadcasted_iota(jnp.int32, (1, width), 1) // LANES
    n_keys = (2 if has_prev else 1) * BLOCK
    zeros_t = jnp.zeros((HEAD_DIM, n_keys), F32)
    nt = (((1,), (1,)), ((), ()))
    log2e = 1.4426950408889634

    def exact_zero(x):
        bits = lax.bitcast_convert_type(x, jnp.uint32)
        return lax.shift_right_logical(lax.shift_right_logical(bits, jnp.uint32(16)), jnp.uint32(16))

    def head_halves(slab, e):
        swap = pltpu.roll(slab, HEAD_DIM, axis=1)
        zero = jnp.zeros_like(slab)
        if e == 0:
            return jnp.where(left, slab, zero).astype(BF16), jnp.where(left, zero, swap).astype(BF16)
        return jnp.where(left, swap, zero).astype(BF16), jnp.where(left, zero, slab).astype(BF16)

    def merged_max(s_prev, s_cur, sink):
        s = jnp.where(from_prev, s_prev if has_prev else -jnp.inf, s_cur)
        return s, jnp.maximum(jnp.max(s, axis=0, keepdims=True), sink)

    def probs(s, m):
        p = jnp.exp2(s - m).astype(BF16)
        return ([p * mprev_ref[...]] if has_prev else []) + [p * mcur_ref[...]]

    sum_row = lax.broadcasted_iota(jnp.int32, (BF16_SUBLANES, 2 * n_keys), 0)
    sum_col = lax.broadcasted_iota(jnp.int32, (BF16_SUBLANES, 2 * n_keys), 1)
    sum_rows = jnp.where((sum_row == 0) == (sum_col < n_keys), 1.0, 0.0) * (sum_row < 2)

    v_t = {}

    def scores(kv_head):
        j, e = divmod(kv_head, 2)
        kcol, vcol = j * LANES, KV_WIDTH + j * LANES
        if e == 0:
            v_rows = ([kvp_ref[:, vcol:vcol + LANES]] if has_prev else []) + [kvc_ref[:, vcol:vcol + LANES]]
            v_t[j] = jnp.concatenate(v_rows, axis=0).astype(F32).T
        kc_l, kc_r = head_halves(kvc_ref[:, kcol:kcol + LANES].astype(F32), e)
        if has_prev:
            kp_l, kp_r = head_halves(kvp_ref[:, kcol:kcol + LANES].astype(F32), e)
            k_rows = jnp.concatenate([kp_l, kc_l, kp_r, kc_r], axis=0)
        else:
            k_rows = jnp.concatenate([kc_l, kc_r], axis=0)
        slabs = [slice((kv_head * pairs + pp) * LANES, (kv_head * pairs + pp + 1) * LANES) for pp in range(pairs)]
        main = q_ref.shape[1]
        q_rows = jnp.concatenate([q_ref[:, sl] if sl.start < main else qt_ref[:, sl.start - main:sl.stop - main]
                                  for sl in slabs], axis=0)
        sink_l = jnp.zeros((1, width), F32)
        sink_r = jnp.zeros((1, width), F32)
        for pp in range(pairs):
            head = 2 * (kv_head * pairs + pp)
            sink_l = jnp.where(lane_pair == pp, sink_ref[head] * log2e, sink_l)
            sink_r = jnp.where(lane_pair == pp, sink_ref[head + 1] * log2e, sink_r)
        s = lax.dot_general(k_rows, q_rows, nt, preferred_element_type=F32)
        tiles = [s[i * BLOCK:(i + 1) * BLOCK] for i in range(s.shape[0] // BLOCK)]
        if has_prev:
            s_l, m_l = merged_max(tiles[0], tiles[1], sink_l)
            s_r, m_r = merged_max(tiles[2], tiles[3], sink_r)
        else:
            s_l, m_l = merged_max(None, tiles[0], sink_l)
            s_r, m_r = merged_max(None, tiles[1], sink_r)
        return dict(vt=v_t[j][e * HEAD_DIM:(e + 1) * HEAD_DIM], slabs=slabs, s_l=s_l, s_r=s_r, m_l=m_l, m_r=m_r,
                    sink_l=sink_l, sink_r=sink_r)

    def finish(st, after):
        vt = st["vt"]
        ones = sum_rows
        if after is not None:
            zero = exact_zero(after[0]) | exact_zero(after[1])
            ones = lax.bitcast_convert_type(lax.bitcast_convert_type(ones, jnp.uint32) + zero[:, :2 * n_keys], F32)
        v_bd = jnp.concatenate([jnp.concatenate([vt, zeros_t], axis=1),
                                jnp.concatenate([zeros_t, vt], axis=1), ones], axis=0).astype(BF16)
        p2 = jnp.concatenate(probs(st["s_l"], st["m_l"]) + probs(st["s_r"], st["m_r"]), axis=0)
        o_t = jnp.dot(v_bd, p2, preferred_element_type=F32)
        d_l = o_t[2 * HEAD_DIM:2 * HEAD_DIM + 1] + jnp.exp2(st["sink_l"] - st["m_l"])
        d_r = o_t[2 * HEAD_DIM + 1:2 * HEAD_DIM + 2] + jnp.exp2(st["sink_r"] - st["m_r"])
        o_t = jnp.concatenate([o_t[:HEAD_DIM] * (1.0 / d_l), o_t[HEAD_DIM:2 * HEAD_DIM] * (1.0 / d_r)], axis=0)
        for pp, sl in enumerate(st["slabs"]):
            o = o_t[:, pp * LANES:(pp + 1) * LANES].T
            o_ref[:, sl] = (o * ga_ref[:, sl].astype(F32)).astype(o_ref.dtype)

    st = scores(0)
    for kv_head in range(N_KV_HEADS):
        nxt = scores(kv_head + 1) if kv_head + 1 < N_KV_HEADS else None
        finish(st, (nxt["m_l"], nxt["m_r"]) if nxt is not None else None)
        st = nxt


def _attn_kernel(sink_ref, q_ref, qt_ref, kvp_ref, kvc_ref, ga_ref, o_ref, mprev_ref, mcur_ref):
    assert WINDOW == BLOCK
    blk = pl.program_id(0) % (SEQ // BLOCK)
    width = mprev_ref.shape[1]
    key = lax.broadcasted_iota(jnp.int32, (BLOCK, width), 0)
    qry = lax.broadcasted_iota(jnp.int32, (BLOCK, width), 1) % LANES
    mprev_ref[...] = (key > qry).astype(BF16)
    mcur_ref[...] = (key <= qry).astype(BF16)
    args = (sink_ref, q_ref, qt_ref, kvp_ref, kvc_ref, ga_ref, o_ref, mprev_ref, mcur_ref)
    pl.when(blk > 0)(lambda: _attn_body(*args, has_prev=True))
    pl.when(blk == 0)(lambda: _attn_body(*args, has_prev=False))


def _attention(q, qkv, act_a, sink):
    t = q.shape[0]
    nb = t // BLOCK
    assert Q_TAIL == 2 * KV_WIDTH
    return pl.pallas_call(
        _attn_kernel,
        grid_spec=pltpu.PrefetchScalarGridSpec(
            num_scalar_prefetch=1,
            grid=(nb,),
            in_specs=[pl.BlockSpec((BLOCK, ATTN_WIDTH - Q_TAIL), lambda i, s: (i, 0)),
                      pl.BlockSpec((BLOCK, Q_TAIL), lambda i, s: (i, 0)),
                      pl.BlockSpec((BLOCK, 2 * KV_WIDTH), lambda i, s: (jnp.maximum(i - 1, 0), 1)),
                      pl.BlockSpec((BLOCK, 2 * KV_WIDTH), lambda i, s: (i, 1)),
                      pl.BlockSpec((BLOCK, ATTN_WIDTH), lambda i, s: (i, 0))],
            out_specs=pl.BlockSpec((BLOCK, ATTN_WIDTH), lambda i, s: (i, 0)),
            scratch_shapes=[pltpu.VMEM((BLOCK, (Q_PER_KV // 2) * LANES), BF16)] * 2,
        ),
        out_shape=jax.ShapeDtypeStruct((t, ATTN_WIDTH), BF16),
        compiler_params=_params("arbitrary"),
        name="swa_sink_attention",
    )(sink, q, qkv, qkv, qkv, act_a)


def _gating_kernel(u_ref, v_ref, gb_ref, lng_ref, lnb_ref, ws_ref, bt_ref, o_ref):
    v = v_ref[...].astype(F32)
    mu = jnp.mean(v, axis=-1, keepdims=True)
    vc = v - mu
    var = jnp.mean(vc * vc, axis=-1, keepdims=True)
    vn = (vc * lax.rsqrt(var + LN_EPS) * lng_ref[...] + lnb_ref[...]).astype(BF16)
    ti = lax.broadcasted_iota(jnp.int32, (GMLP_CHUNK, GMLP_CHUNK), 0)
    si = lax.broadcasted_iota(jnp.int32, (GMLP_CHUNK, GMLP_CHUNK), 1)
    causal = si <= ti
    bt = bt_ref[...]
    for g in range(GMLP_GROUPS):
        w = jnp.where(causal, ws_ref[g], 0.0).astype(BF16)
        bias = bt[:, g:g + 1]
        cols = slice(g * GMLP_GROUP_DIM, (g + 1) * GMLP_GROUP_DIM)
        for c in range(v.shape[0] // GMLP_CHUNK):
            rows = slice(c * GMLP_CHUNK, (c + 1) * GMLP_CHUNK)
            mixed = jnp.dot(w, vn[rows, cols], preferred_element_type=F32) + bias
            sg = u_ref[rows, cols].astype(F32) * mixed
            o_ref[rows, cols] = (sg * gb_ref[rows, cols].astype(F32)).astype(o_ref.dtype)


def _spatial_gating(act_uv, act_gate, ln_g, ln_b, w_s, b_t):
    t = act_uv.shape[0]
    r = 2 * GMLP_CHUNK
    w = GMLP_WIDTH
    return pl.pallas_call(
        _gating_kernel,
        grid=(t // r,),
        in_specs=[pl.BlockSpec((r, w), lambda i: (i, 0)),
                  pl.BlockSpec((r, w), lambda i: (i, 1)),
                  pl.BlockSpec((r, w), lambda i: (i, 1)),
                  pl.BlockSpec((1, w), lambda i: (0, 0)),
                  pl.BlockSpec((1, w), lambda i: (0, 0)),
                  pl.BlockSpec((GMLP_GROUPS, GMLP_CHUNK, GMLP_CHUNK), lambda i: (0, 0, 0)),
                  pl.BlockSpec((GMLP_CHUNK, GMLP_GROUPS), lambda i: (0, 0))],
        out_specs=pl.BlockSpec((r, w), lambda i: (i, 0)),
        out_shape=jax.ShapeDtypeStruct((t, w), BF16),
        compiler_params=_params("arbitrary"),
        name="spatial_gating",
    )(act_uv, act_uv, act_gate, ln_g, ln_b, w_s, b_t)


def _out_kernel(a_ref, w_ref, x_ref, g_ref, o_ref, y0_ref, y1_ref, ss0_ref, ss1_ref):
    i, j = pl.program_id(0), pl.program_id(1)
    col = pl.multiple_of(j * TN, TN)

    @pl.when((i == 0) & (j == 0))
    def _():
        y1_ref[...] = jnp.zeros_like(y1_ref)
        ss1_ref[...] = jnp.zeros_like(ss1_ref)

    def step(y_ref, ss_ref, yp_ref, ssp_ref):
        ms_prev = jnp.sum(ssp_ref[...], axis=-1, keepdims=True) * (1.0 / D_MODEL)
        o_ref[...] = yp_ref[:, pl.ds(col, TN)] * lax.rsqrt(ms_prev + NORM_EPS) * g_ref[:, pl.ds(col, TN)]
        y = x_ref[...] + jnp.dot(a_ref[...], w_ref[...], preferred_element_type=F32)
        sq = y * y
        part = sq[:, :LANES]
        for s in range(1, TN // LANES):
            part = part + sq[:, s * LANES:(s + 1) * LANES]
        ss_ref[...] = jnp.where(j == 0, part, ss_ref[...] + part)
        y_ref[:, pl.ds(col, TN)] = y

    pl.when(i % 2 == 0)(lambda: step(y0_ref, ss0_ref, y1_ref, ss1_ref))
    pl.when(i % 2 == 1)(lambda: step(y1_ref, ss1_ref, y0_ref, ss0_ref))


def _out_proj(a, w, x2, g_row):
    t, k = a.shape
    n = w.shape[1]
    last = t // TM_OUT - 1
    return pl.pallas_call(
        _out_kernel,
        grid=(t // TM_OUT + 1, n // TN),
        in_specs=[pl.BlockSpec((TM_OUT, k), lambda i, j: (jnp.minimum(i, last), 0)),
                  pl.BlockSpec((k, TN), lambda i, j: (0, j)),
                  pl.BlockSpec((TM_OUT, TN), lambda i, j: (jnp.minimum(i, last), j)),
                  pl.BlockSpec((1, n), lambda i, j: (0, 0))],
        out_specs=pl.BlockSpec((TM_OUT, TN), lambda i, j: (jnp.maximum(i - 1, 0), jnp.where(i == 0, 0, j))),
        out_shape=jax.ShapeDtypeStruct((t, n), F32),
        scratch_shapes=[pltpu.VMEM((TM_OUT, n), F32), pltpu.VMEM((TM_OUT, n), F32),
                        pltpu.VMEM((TM_OUT, LANES), F32), pltpu.VMEM((TM_OUT, LANES), F32)],
        compiler_params=_params("arbitrary", "arbitrary"),
        name="out_proj_residual_norm",
    )(a, w, x2, g_row)


def _rope_inv_freq_rows():
    inv_freq = ROPE_THETA ** (-jnp.arange(ROPE_HALF, dtype=F32) * 2.0 / ROPE_DIM)
    return jnp.broadcast_to(inv_freq[:, None], (ROPE_HALF, LANES))


def kernel(x, positions, norm_g, w_in, attn_sink, gmlp_ln_g, gmlp_ln_b, w_spatial, b_spatial, w_up_attn,
           w_up_gmlp, w_out, final_norm_g):
    b, s, d = x.shape
    t = b * s
    assert s == SEQ and d == D_MODEL and norm_g.shape[0] == DEPTH == 1
    x2 = x.reshape(t, d)
    cos_t, sin_t = _rope_tables(positions.reshape(t // LANES, LANES), _rope_inv_freq_rows())
    nw = D_MODEL // TN
    w_i = w_in[0]
    qkv_w = Q_TAIL + 2 * KV_WIDTH
    qkv, h = _matmul("norm_in_proj_qkv", x2, w_i, _epilogue_qkv, [OFF_KV - Q_TAIL], qkv_w, TM_NORM, qkv_w,
                     lane_tiles=(cos_t, sin_t), m_chunks=NORM_CHUNKS, norm_gain=norm_g)
    q = _matmul("in_proj_q", h, w_i, _epilogue_q, [OFF_Q], ATTN_WIDTH - Q_TAIL, TM, TN, lane_tiles=(cos_t, sin_t),
                m_chunks=Q_CHUNKS)
    act_gate = _matmul("in_proj_silu", h, w_i, functools.partial(_epilogue_act, act=_silu),
                       [OFF_GATE_A, OFF_GATE_B], ATTN_WIDTH, TM, TN)
    act_uv = _matmul("in_proj_gelu", h, w_i, functools.partial(_epilogue_act, act=jax.nn.gelu),
                     [OFF_U], 2 * GMLP_WIDTH, TM, TN)
    act_merge = _matmul("in_proj_sigmoid", h, w_i, functools.partial(_epilogue_act, act=_sigmoid),
                        [OFF_MERGE], 2 * D_MODEL, TM, TN)
    attn_g = _attention(q, qkv, act_gate, attn_sink[0])
    ya, w_out_b = _matmul("up_attn", attn_g, w_up_attn[0], _epilogue_up_attn, [0], D_MODEL, TM, TN,
                          tiles=[(act_merge, 0)], side_convert=w_out[0])
    sgb = _spatial_gating(act_uv, act_gate, gmlp_ln_g, gmlp_ln_b, w_spatial[0], b_spatial[0].T)
    merged = _matmul("up_gmlp_merge", sgb, w_up_gmlp[0], _epilogue_up_gmlp, [0], D_MODEL, TM, TN,
                     tiles=[(act_merge, nw), (ya, 0)])
    out = _out_proj(merged, w_out_b, x2, final_norm_g[None, :])
    return out.reshape(b, s, d)
```

```python
import functools

import jax
import jax.numpy as jnp
from jax import lax
from jax.experimental import pallas as pl
from jax.experimental.pallas import tpu as pltpu

D_MODEL = 4096
SEQ = 4096
DEPTH = 1
N_Q_HEADS = 64
N_KV_HEADS = 8
HEAD_DIM = 64
Q_PER_KV = N_Q_HEADS // N_KV_HEADS
ATTN_WIDTH = N_Q_HEADS * HEAD_DIM
KV_WIDTH = N_KV_HEADS * HEAD_DIM
WINDOW = 128
BLOCK = 128
ROPE_THETA = 500000.0
ROPE_DIM = HEAD_DIM // 4
ROPE_HALF = ROPE_DIM // 2
GMLP_WIDTH = D_MODEL
GMLP_GROUPS = 8
GMLP_GROUP_DIM = GMLP_WIDTH // GMLP_GROUPS
GMLP_CHUNK = 128
NORM_EPS = 1e-5
LN_EPS = 1e-5

OFF_Q = 0
OFF_KV = ATTN_WIDTH
OFF_GATE_A = OFF_KV + 2 * KV_WIDTH
OFF_U = OFF_GATE_A + ATTN_WIDTH
OFF_GATE_B = OFF_U + 2 * GMLP_WIDTH
OFF_MERGE = OFF_GATE_B + GMLP_WIDTH

LANES = 128
BF16_SUBLANES = 16
VMEM_LIMIT = 56 * 1024 * 1024

TM = 1024
TN = 1024
Q_CHUNKS = 4
TM_NORM = 512
Q_TAIL = 1024
NORM_CHUNKS = 2
TM_OUT = 512

BF16 = jnp.bfloat16
F32 = jnp.float32


def _params(*sem):
    return pltpu.CompilerParams(dimension_semantics=sem, vmem_limit_bytes=VMEM_LIMIT)


def _rope_table_kernel(pos_ref, invf_ref, cos_ref, sin_ref):
    pad = jnp.zeros((HEAD_DIM - ROPE_DIM, LANES), F32)
    for g in range(pos_ref.shape[0]):
        ang = pos_ref[g:g + 1, :].astype(F32) * invf_ref[...]
        for fn, out_ref in ((jnp.cos, cos_ref), (jnp.sin, sin_ref)):
            val = fn(ang)
            head = jnp.concatenate([val, val, pad], axis=0)
            out_ref[g * LANES:(g + 1) * LANES, :] = jnp.concatenate([head, head], axis=0).T


def _rope_tables(pos_rows, invf_rows):
    groups = 8
    n = pos_rows.shape[0]
    return pl.pallas_call(
        _rope_table_kernel,
        grid=(n // groups,),
        in_specs=[pl.BlockSpec((groups, LANES), lambda i: (i, 0)),
                  pl.BlockSpec((ROPE_HALF, LANES), lambda i: (0, 0))],
        out_specs=[pl.BlockSpec((groups * LANES, LANES), lambda i: (i, 0)),
                   pl.BlockSpec((groups * LANES, LANES), lambda i: (i, 0))],
        out_shape=[jax.ShapeDtypeStruct((n * LANES, LANES), F32)] * 2,
        compiler_params=_params("arbitrary"),
        name="rope_tables",
    )(pos_rows, invf_rows)


def _rope_slab(a, cos, sin, lo, hi):
    up = pltpu.roll(a, LANES - ROPE_HALF, axis=1)
    dn = pltpu.roll(a, ROPE_HALF, axis=1)
    return jnp.where(lo, a * cos - up * sin, jnp.where(hi, a * cos + dn * sin, a))


def _rope_masks(rows):
    j = lax.broadcasted_iota(jnp.int32, (rows, LANES), 1) & (HEAD_DIM - 1)
    return j < ROPE_HALF, (j >= ROPE_HALF) & (j < ROPE_DIM)


def _epilogue_q(acc, o_ref, cos_ref, sin_ref):
    cos, sin = cos_ref[...], sin_ref[...]
    lo, hi = _rope_masks(acc.shape[0])
    scale = HEAD_DIM ** -0.5 * 1.4426950408889634
    for s in range(acc.shape[1] // LANES):
        sl = slice(s * LANES, (s + 1) * LANES)
        o_ref[:, sl] = (_rope_slab(acc[:, sl], cos, sin, lo, hi) * scale).astype(o_ref.dtype)


def _epilogue_qkv(acc, o_ref, cos_ref, sin_ref):
    cos, sin = cos_ref[...], sin_ref[...]
    lo, hi = _rope_masks(acc.shape[0])
    scale = HEAD_DIM ** -0.5 * 1.4426950408889634
    for s in range(acc.shape[1] // LANES):
        sl = slice(s * LANES, (s + 1) * LANES)
        a = acc[:, sl]
        if s * LANES < Q_TAIL:
            a = _rope_slab(a, cos, sin, lo, hi) * scale
        elif s * LANES < Q_TAIL + KV_WIDTH:
            a = _rope_slab(a, cos, sin, lo, hi)
        o_ref[:, sl] = a.astype(o_ref.dtype)


def _sigmoid(x):
    return 0.5 * jnp.tanh(0.5 * x) + 0.5


def _silu(x):
    half = 0.5 * x
    return half * jnp.tanh(half) + half


def _gelu(x):
    c, a = 0.7978845608028654, 0.044715
    half = 0.5 * x
    return half * jnp.tanh(x * (x * x * (c * a) + c)) + half


def _epilogue_act(acc, o_ref, *, act):
    for s in range(acc.shape[1] // LANES):
        sl = slice(s * LANES, (s + 1) * LANES)
        o_ref[:, sl] = act(acc[:, sl]).astype(o_ref.dtype)


def _epilogue_gelu_gated(acc, o_ref, gate_ref):
    for s in range(acc.shape[1] // LANES):
        sl = slice(s * LANES, (s + 1) * LANES)
        o_ref[:, sl] = (_gelu(acc[:, sl]) * gate_ref[:, sl].astype(F32)).astype(o_ref.dtype)


def _epilogue_up_attn(acc, o_ref, sa_ref):
    for s in range(acc.shape[1] // LANES):
        sl = slice(s * LANES, (s + 1) * LANES)
        o_ref[:, sl] = (sa_ref[:, sl].astype(F32) * acc[:, sl]).astype(o_ref.dtype)


def _epilogue_up_gmlp(acc, o_ref, sb_ref, ya_ref):
    for s in range(acc.shape[1] // LANES):
        sl = slice(s * LANES, (s + 1) * LANES)
        merged = ya_ref[:, sl].astype(F32) + sb_ref[:, sl].astype(F32) * acc[:, sl]
        o_ref[:, sl] = merged.astype(o_ref.dtype)


def _matmul_kernel(a_ref, w_hbm, *rest, epilogue, n_extra, w_col, nb, mb, m_chunks, with_norm, with_side):
    if with_norm:
        g_ref, rest = rest[0], rest[1:]
    extras, rest = rest[:n_extra], rest[n_extra:]
    if with_side:
        side_in_ref, rest = rest[0], rest[1:]
    o_ref, rest = rest[0], rest[1:]
    if with_norm:
        h_ref, rest = rest[0], rest[1:]
    if with_side:
        side_out_ref, rest = rest[0], rest[1:]
        side_out_ref[...] = side_in_ref[...].astype(BF16)
    wb_ref, stage_ref, sem = rest
    n, m = pl.program_id(0), pl.program_id(1)
    k, tn = wb_ref.shape[1:]
    kc = k // mb
    cur = n % 2

    def chunk_copy(block, c, slot):
        col = pl.multiple_of(w_col(block), LANES)
        return pltpu.make_async_copy(w_hbm.at[pl.ds(c * kc, kc), pl.ds(col, tn)], stage_ref.at[slot], sem.at[slot])

    @pl.when((n == 0) & (m == 0))
    def _():
        chunk_copy(0, 0, 0).start()
        for c in range(mb):
            if c + 1 < mb:
                chunk_copy(0, c + 1, (c + 1) % 2).start()
            chunk_copy(0, c, c % 2).wait()
            wb_ref[0, c * kc:(c + 1) * kc, :] = stage_ref[c % 2].astype(BF16)

    if nb > 1:
        @pl.when(n + 1 < nb)
        def _():
            chunk_copy(n + 1, m, m % 2).start()

        @pl.when((n + 1 < nb) & (m > 0))
        def _():
            chunk_copy(n + 1, m - 1, (m - 1) % 2).wait()
            row = pl.multiple_of((m - 1) * kc, kc)
            wb_ref[1 - cur, pl.ds(row, kc), :] = stage_ref[(m - 1) % 2].astype(BF16)

        @pl.when((n > 0) & (m == 0))
        def _():
            chunk_copy(n, mb - 1, (mb - 1) % 2).wait()
            wb_ref[cur, (mb - 1) * kc:, :] = stage_ref[(mb - 1) % 2].astype(BF16)

    rows = a_ref.shape[0] // m_chunks
    for c in range(m_chunks):
        rs = pl.ds(c * rows, rows)
        lhs = a_ref[rs, :]
        if with_norm:
            ms = jnp.mean(lhs * lhs, axis=-1, keepdims=True)
            lhs = (lhs * lax.rsqrt(ms + NORM_EPS) * g_ref[...]).astype(BF16)
            h_ref[rs, :] = lhs
        acc = jnp.dot(lhs, wb_ref[cur], preferred_element_type=F32)
        epilogue(acc, o_ref.at[rs, :], *[e.at[rs, :] for e in extras])


def _matmul(name, a, w, epilogue, col_ranges, width, tm, tn, lane_tiles=(), tiles=(), m_chunks=1,
            norm_gain=None, side_convert=None):
    t, k = a.shape
    per_range = width // tn
    nb, mb = per_range * len(col_ranges), t // tm
    with_norm = norm_gain is not None
    with_side = side_convert is not None
    assert not with_norm or nb == 1

    def w_col(n):
        col = col_ranges[0]
        for r in range(1, len(col_ranges)):
            col = jnp.where(n // per_range == r, col_ranges[r], col)
        return col + n % per_range * tn

    extra_specs = [pl.BlockSpec((tm, LANES), lambda n, m: (m, 0)) for _ in lane_tiles]
    extra_specs += [pl.BlockSpec((tm, tn), functools.partial(lambda n, m, off: (m, n + off), off=off))
                    for _, off in tiles]
    extras = list(lane_tiles) + [arr for arr, _ in tiles]
    body = functools.partial(_matmul_kernel, epilogue=epilogue, n_extra=len(extras), w_col=w_col, nb=nb, mb=mb,
                             m_chunks=m_chunks, with_norm=with_norm, with_side=with_side)
    in_specs = [pl.BlockSpec((tm, k), lambda n, m: (m, 0)), pl.BlockSpec(memory_space=pl.ANY)]
    out_specs = [pl.BlockSpec((tm, tn), lambda n, m: (m, n))]
    out_shape = [jax.ShapeDtypeStruct((t, nb * tn), BF16)]
    operands = [a, w]
    if with_norm:
        in_specs.append(pl.BlockSpec((1, k), lambda n, m: (0, 0)))
        operands.append(norm_gain)
        out_specs.append(pl.BlockSpec((tm, k), lambda n, m: (m, 0)))
        out_shape.append(jax.ShapeDtypeStruct((t, k), BF16))
    in_specs += extra_specs
    operands += extras
    if with_side:
        slab = (side_convert.shape[0] // (nb * mb), side_convert.shape[1])
        in_specs.append(pl.BlockSpec(slab, lambda n, m: (n * mb + m, 0)))
        operands.append(side_convert)
        out_specs.append(pl.BlockSpec(slab, lambda n, m: (n * mb + m, 0)))
        out_shape.append(jax.ShapeDtypeStruct(side_convert.shape, BF16))
    outs = pl.pallas_call(
        body,
        grid=(nb, mb),
        in_specs=in_specs,
        out_specs=out_specs,
        out_shape=out_shape,
        scratch_shapes=[pltpu.VMEM((min(nb, 2), k, tn), BF16), pltpu.VMEM((2, k // mb, tn), F32),
                        pltpu.SemaphoreType.DMA((2,))],
        compiler_params=_params("arbitrary", "arbitrary"),
        name=name,
    )(*operands)
    return outs if len(outs) > 1 else outs[0]


def _attn_body(sink_ref, q_ref, qt_ref, kvp_ref, kvc_ref, ga_ref, o_ref, mprev_ref, mcur_ref, has_prev):
    pairs = Q_PER_KV // 2
    width = pairs * LANES
    key = lax.broadcasted_iota(jnp.int32, (BLOCK, width), 0)
    qry = lax.broadcasted_iota(jnp.int32, (BLOCK, width), 1) % LANES
    from_prev = key > qry
    left = lax.broadcasted_iota(jnp.int32, (BLOCK, LANES), 1) < HEAD_DIM
    lane_pair = lax.broadcasted_iota(jnp.int32, (1, width), 1) // LANES
    n_keys = (2 if has_prev else 1) * BLOCK
    zeros_t = jnp.zeros((HEAD_DIM, n_keys), F32)
    nt = (((1,), (1,)), ((), ()))
    log2e = 1.4426950408889634

    def exact_zero(x):
        bits = lax.bitcast_convert_type(x, jnp.uint32)
        return lax.shift_right_logical(lax.shift_right_logical(bits, jnp.uint32(16)), jnp.uint32(16))

    def head_halves(slab, e):
        swap = pltpu.roll(slab, HEAD_DIM, axis=1)
        zero = jnp.zeros_like(slab)
        if e == 0:
            return jnp.where(left, slab, zero).astype(BF16), jnp.where(left, zero, swap).astype(BF16)
        return jnp.where(left, swap, zero).astype(BF16), jnp.where(left, zero, slab).astype(BF16)

    def merged_max(s_prev, s_cur, sink):
        s = jnp.where(from_prev, s_prev if has_prev else -jnp.inf, s_cur)
        return s, jnp.maximum(jnp.max(s, axis=0, keepdims=True), sink)

    def probs(s, m):
        p = jnp.exp2(s - m).astype(BF16)
        return ([p * mprev_ref[...]] if has_prev else []) + [p * mcur_ref[...]]

    sum_row = lax.broadcasted_iota(jnp.int32, (BF16_SUBLANES, 2 * n_keys), 0)
    sum_col = lax.broadcasted_iota(jnp.int32, (BF16_SUBLANES, 2 * n_keys), 1)
    sum_rows = jnp.where((sum_row == 0) == (sum_col < n_keys), 1.0, 0.0) * (sum_row < 2)

    v_t = {}

    def scores(kv_head):
        j, e = divmod(kv_head, 2)
        kcol, vcol = j * LANES, KV_WIDTH + j * LANES
        if e == 0:
            v_rows = ([kvp_ref[:, vcol:vcol + LANES]] if has_prev else []) + [kvc_ref[:, vcol:vcol + LANES]]
            v_t[j] = jnp.concatenate(v_rows, axis=0).astype(F32).T
        kc_l, kc_r = head_halves(kvc_ref[:, kcol:kcol + LANES].astype(F32), e)
        if has_prev:
            kp_l, kp_r = head_halves(kvp_ref[:, kcol:kcol + LANES].astype(F32), e)
            k_rows = jnp.concatenate([kp_l, kc_l, kp_r, kc_r], axis=0)
        else:
            k_rows = jnp.concatenate([kc_l, kc_r], axis=0)
        slabs = [slice((kv_head * pairs + pp) * LANES, (kv_head * pairs + pp + 1) * LANES) for pp in range(pairs)]
        main = q_ref.shape[1]
        q_rows = jnp.concatenate([q_ref[:, sl] if sl.start < main else qt_ref[:, sl.start - main:sl.stop - main]
                                  for sl in slabs], axis=0)
        sink_l = jnp.zeros((1, width), F32)
        sink_r = jnp.zeros((1, width), F32)
        for pp in range(pairs):
            head = 2 * (kv_head * pairs + pp)
            sink_l = jnp.where(lane_pair == pp, sink_ref[head] * log2e, sink_l)
            sink_r = jnp.where(lane_pair == pp, sink_ref[head + 1] * log2e, sink_r)
        s = lax.dot_general(k_rows, q_rows, nt, preferred_element_type=F32)
        tiles = [s[i * BLOCK:(i + 1) * BLOCK] for i in range(s.shape[0] // BLOCK)]
        if has_prev:
            s_l, m_l = merged_max(tiles[0], tiles[1], sink_l)
            s_r, m_r = merged_max(tiles[2], tiles[3], sink_r)
        else:
            s_l, m_l = merged_max(None, tiles[0], sink_l)
            s_r, m_r = merged_max(None, tiles[1], sink_r)
        return dict(vt=v_t[j][e * HEAD_DIM:(e + 1) * HEAD_DIM], slabs=slabs, s_l=s_l, s_r=s_r, m_l=m_l, m_r=m_r,
                    sink_l=sink_l, sink_r=sink_r)

    def finish(st, after):
        vt = st["vt"]
        ones = sum_rows
        if after is not None:
            zero = exact_zero(after[0]) | exact_zero(after[1])
            ones = lax.bitcast_convert_type(lax.bitcast_convert_type(ones, jnp.uint32) + zero[:, :2 * n_keys], F32)
        v_bd = jnp.concatenate([jnp.concatenate([vt, zeros_t], axis=1),
                                jnp.concatenate([zeros_t, vt], axis=1), ones], axis=0).astype(BF16)
        p2 = jnp.concatenate(probs(st["s_l"], st["m_l"]) + probs(st["s_r"], st["m_r"]), axis=0)
        o_t = jnp.dot(v_bd, p2, preferred_element_type=F32)
        d_l = o_t[2 * HEAD_DIM:2 * HEAD_DIM + 1] + jnp.exp2(st["sink_l"] - st["m_l"])
        d_r = o_t[2 * HEAD_DIM + 1:2 * HEAD_DIM + 2] + jnp.exp2(st["sink_r"] - st["m_r"])
        o_t = jnp.concatenate([o_t[:HEAD_DIM] * (1.0 / d_l), o_t[HEAD_DIM:2 * HEAD_DIM] * (1.0 / d_r)], axis=0)
        for pp, sl in enumerate(st["slabs"]):
            o = o_t[:, pp * LANES:(pp + 1) * LANES].T
            o_ref[:, sl] = (o * ga_ref[:, sl].astype(F32)).astype(o_ref.dtype)

    st = scores(0)
    for kv_head in range(N_KV_HEADS):
        nxt = scores(kv_head + 1) if kv_head + 1 < N_KV_HEADS else None
        finish(st, (nxt["m_l"], nxt["m_r"]) if nxt is not None else None)
        st = nxt


def _attn_kernel(sink_ref, q_ref, qt_ref, kvp_ref, kvc_ref, ga_ref, o_ref, mprev_ref, mcur_ref):
    assert WINDOW == BLOCK
    blk = pl.program_id(0) % (SEQ // BLOCK)
    width = mprev_ref.shape[1]
    key = lax.broadcasted_iota(jnp.int32, (BLOCK, width), 0)
    qry = lax.broadcasted_iota(jnp.int32, (BLOCK, width), 1) % LANES
    mprev_ref[...] = (key > qry).astype(BF16)
    mcur_ref[...] = (key <= qry).astype(BF16)
    args = (sink_ref, q_ref, qt_ref, kvp_ref, kvc_ref, ga_ref, o_ref, mprev_ref, mcur_ref)
    pl.when(blk > 0)(lambda: _attn_body(*args, has_prev=True))
    pl.when(blk == 0)(lambda: _attn_body(*args, has_prev=False))


def _attention(q, qkv, act_a, sink):
    t = q.shape[0]
    nb = t // BLOCK
    assert Q_TAIL == 2 * KV_WIDTH
    return pl.pallas_call(
        _attn_kernel,
        grid_spec=pltpu.PrefetchScalarGridSpec(
            num_scalar_prefetch=1,
            grid=(nb,),
            in_specs=[pl.BlockSpec((BLOCK, ATTN_WIDTH - Q_TAIL), lambda i, s: (i, 0)),
                      pl.BlockSpec((BLOCK, Q_TAIL), lambda i, s: (i, 0)),
                      pl.BlockSpec((BLOCK, 2 * KV_WIDTH), lambda i, s: (jnp.maximum(i - 1, 0), 1)),
                      pl.BlockSpec((BLOCK, 2 * KV_WIDTH), lambda i, s: (i, 1)),
                      pl.BlockSpec((BLOCK, ATTN_WIDTH), lambda i, s: (i, 0))],
            out_specs=pl.BlockSpec((BLOCK, ATTN_WIDTH), lambda i, s: (i, 0)),
            scratch_shapes=[pltpu.VMEM((BLOCK, (Q_PER_KV // 2) * LANES), BF16)] * 2,
        ),
        out_shape=jax.ShapeDtypeStruct((t, ATTN_WIDTH), BF16),
        compiler_params=_params("arbitrary"),
        name="swa_sink_attention",
    )(sink, q, qkv, qkv, qkv, act_a)


def _gating_kernel(u_ref, v_ref, lng_ref, lnb_ref, ws_ref, bt_ref, o_ref):
    v = v_ref[...].astype(F32)
    mu = jnp.mean(v, axis=-1, keepdims=True)
    vc = v - mu
    var = jnp.mean(vc * vc, axis=-1, keepdims=True)
    vn = (vc * lax.rsqrt(var + LN_EPS) * lng_ref[...] + lnb_ref[...]).astype(BF16)
    ti = lax.broadcasted_iota(jnp.int32, (GMLP_CHUNK, GMLP_CHUNK), 0)
    si = lax.broadcasted_iota(jnp.int32, (GMLP_CHUNK, GMLP_CHUNK), 1)
    causal = si <= ti
    bt = bt_ref[...]
    for g in range(GMLP_GROUPS):
        w = jnp.where(causal, ws_ref[g], 0.0).astype(BF16)
        bias = bt[:, g:g + 1]
        cols = slice(g * GMLP_GROUP_DIM, (g + 1) * GMLP_GROUP_DIM)
        for c in range(v.shape[0] // GMLP_CHUNK):
            rows = slice(c * GMLP_CHUNK, (c + 1) * GMLP_CHUNK)
            mixed = jnp.dot(w, vn[rows, cols], preferred_element_type=F32) + bias
            o_ref[rows, cols] = (u_ref[rows, cols].astype(F32) * mixed).astype(o_ref.dtype)


def _spatial_gating(act_u, act_v, ln_g, ln_b, w_s, b_t):
    t = act_u.shape[0]
    r = 2 * GMLP_CHUNK
    w = GMLP_WIDTH
    return pl.pallas_call(
        _gating_kernel,
        grid=(t // r,),
        in_specs=[pl.BlockSpec((r, w), lambda i: (i, 0)),
                  pl.BlockSpec((r, w), lambda i: (i, 0)),
                  pl.BlockSpec((1, w), lambda i: (0, 0)),
                  pl.BlockSpec((1, w), lambda i: (0, 0)),
                  pl.BlockSpec((GMLP_GROUPS, GMLP_CHUNK, GMLP_CHUNK), lambda i: (0, 0, 0)),
                  pl.BlockSpec((GMLP_CHUNK, GMLP_GROUPS), lambda i: (0, 0))],
        out_specs=pl.BlockSpec((r, w), lambda i: (i, 0)),
        out_shape=jax.ShapeDtypeStruct((t, w), BF16),
        compiler_params=_params("arbitrary"),
        name="spatial_gating",
    )(act_u, act_v, ln_g, ln_b, w_s, b_t)


def _out_kernel(a_ref, w_ref, x_ref, g_ref, o_ref, y_ref, ss_ref, *, n_row_blocks):
    i, j = pl.program_id(0), pl.program_id(1)
    col = pl.multiple_of(j * TN, TN)
    cur = i % 2

    @pl.when(i > 0)
    def _():
        ms_prev = jnp.sum(ss_ref[1 - cur], axis=-1, keepdims=True) * (1.0 / D_MODEL)
        o_ref[...] = y_ref[:, pl.ds(col, TN)] * lax.rsqrt(ms_prev + NORM_EPS) * g_ref[:, pl.ds(col, TN)]

    @pl.when(i < n_row_blocks)
    def _():
        y = x_ref[...] + jnp.dot(a_ref[...], w_ref[...], preferred_element_type=F32)
        sq = y * y
        part = sq[:, :LANES]
        for s in range(1, TN // LANES):
            part = part + sq[:, s * LANES:(s + 1) * LANES]
        ss_ref[cur] = jnp.where(j == 0, part, ss_ref[cur] + part)
        y_ref[:, pl.ds(col, TN)] = y


def _out_proj(a, w, x2, g_row):
    t, k = a.shape
    n = w.shape[1]
    last = t // TM_OUT - 1
    return pl.pallas_call(
        functools.partial(_out_kernel, n_row_blocks=last + 1),
        grid=(t // TM_OUT + 1, n // TN),
        in_specs=[pl.BlockSpec((TM_OUT, k), lambda i, j: (jnp.minimum(i, last), 0)),
                  pl.BlockSpec((k, TN), lambda i, j: (0, j)),
                  pl.BlockSpec((TM_OUT, TN), lambda i, j: (jnp.minimum(i, last), j)),
                  pl.BlockSpec((1, n), lambda i, j: (0, 0))],
        out_specs=pl.BlockSpec((TM_OUT, TN), lambda i, j: (jnp.maximum(i - 1, 0), jnp.where(i == 0, 0, j))),
        out_shape=jax.ShapeDtypeStruct((t, n), F32),
        scratch_shapes=[pltpu.VMEM((TM_OUT, n), F32), pltpu.VMEM((2, TM_OUT, LANES), F32)],
        compiler_params=_params("arbitrary", "arbitrary"),
        name="out_proj_residual_norm",
    )(a, w, x2, g_row)


def _rope_inv_freq_rows():
    inv_freq = ROPE_THETA ** (-jnp.arange(ROPE_HALF, dtype=F32) * 2.0 / ROPE_DIM)
    return jnp.broadcast_to(inv_freq[:, None], (ROPE_HALF, LANES))


def kernel(x, positions, norm_g, w_in, attn_sink, gmlp_ln_g, gmlp_ln_b, w_spatial, b_spatial, w_up_attn,
           w_up_gmlp, w_out, final_norm_g):
    b, s, d = x.shape
    t = b * s
    assert s == SEQ and d == D_MODEL and norm_g.shape[0] == DEPTH == 1
    x2 = x.reshape(t, d)
    cos_t, sin_t = _rope_tables(positions.reshape(t // LANES, LANES), _rope_inv_freq_rows())
    nw = D_MODEL // TN
    w_i = w_in[0]
    qkv_w = Q_TAIL + 2 * KV_WIDTH
    qkv, h = _matmul("norm_in_proj_qkv", x2, w_i, _epilogue_qkv, [OFF_KV - Q_TAIL], qkv_w, TM_NORM, qkv_w,
                     lane_tiles=(cos_t, sin_t), m_chunks=NORM_CHUNKS, norm_gain=norm_g)
    q = _matmul("in_proj_q", h, w_i, _epilogue_q, [OFF_Q], ATTN_WIDTH - Q_TAIL, TM, TN, lane_tiles=(cos_t, sin_t),
                m_chunks=Q_CHUNKS)
    act_gate = _matmul("in_proj_silu", h, w_i, functools.partial(_epilogue_act, act=_silu),
                       [OFF_GATE_A, OFF_GATE_B], ATTN_WIDTH, TM, TN)
    act_u = _matmul("in_proj_gelu_u", h, w_i, _epilogue_gelu_gated, [OFF_U], GMLP_WIDTH, TM, TN,
                    tiles=[(act_gate, nw)])
    act_v = _matmul("in_proj_gelu_v", h, w_i, functools.partial(_epilogue_act, act=_gelu),
                    [OFF_U + GMLP_WIDTH], GMLP_WIDTH, TM, TN)
    act_merge = _matmul("in_proj_sigmoid", h, w_i, functools.partial(_epilogue_act, act=_sigmoid),
                        [OFF_MERGE], 2 * D_MODEL, TM, TN)
    attn_g = _attention(q, qkv, act_gate, attn_sink[0])
    ya, w_out_b = _matmul("up_attn", attn_g, w_up_attn[0], _epilogue_up_attn, [0], D_MODEL, TM, TN,
                          tiles=[(act_merge, 0)], side_convert=w_out[0])
    sgb = _spatial_gating(act_u, act_v, gmlp_ln_g, gmlp_ln_b, w_spatial[0], b_spatial[0].T)
    merged = _matmul("up_gmlp_merge", sgb, w_up_gmlp[0], _epilogue_up_gmlp, [0], D_MODEL, TM, TN,
                     tiles=[(act_merge, nw), (ya, 0)])
    out = _out_proj(merged, w_out_b, x2, final_norm_g[None, :])
    return out.reshape(b, s, d)
```

```python
import functools

import jax
import jax.numpy as jnp
from jax import lax
from jax.experimental import pallas as pl
from jax.experimental.pallas import tpu as pltpu

D_MODEL = 4096
SEQ = 4096
DEPTH = 1
N_Q_HEADS = 64
N_KV_HEADS = 8
HEAD_DIM = 64
Q_PER_KV = N_Q_HEADS // N_KV_HEADS
ATTN_WIDTH = N_Q_HEADS * HEAD_DIM
KV_WIDTH = N_KV_HEADS * HEAD_DIM
WINDOW = 128
BLOCK = 128
ROPE_THETA = 500000.0
ROPE_DIM = HEAD_DIM // 4
ROPE_HALF = ROPE_DIM // 2
GMLP_WIDTH = D_MODEL
GMLP_GROUPS = 8
GMLP_GROUP_DIM = GMLP_WIDTH // GMLP_GROUPS
GMLP_CHUNK = 128
NORM_EPS = 1e-5
LN_EPS = 1e-5

OFF_Q = 0
OFF_KV = ATTN_WIDTH
OFF_GATE_A = OFF_KV + 2 * KV_WIDTH
OFF_U = OFF_GATE_A + ATTN_WIDTH
OFF_GATE_B = OFF_U + 2 * GMLP_WIDTH
OFF_MERGE = OFF_GATE_B + GMLP_WIDTH

LANES = 128
BF16_SUBLANES = 16
VMEM_LIMIT = 56 * 1024 * 1024

TM = 1024
TN = 1024
Q_CHUNKS = 4
TM_NORM = 512
Q_TAIL = 1024
NORM_CHUNKS = 2
TM_OUT = 1024
TN_OUT = 512

BF16 = jnp.bfloat16
F32 = jnp.float32


def _params(*sem):
    return pltpu.CompilerParams(dimension_semantics=sem, vmem_limit_bytes=VMEM_LIMIT)


def _rope_table_kernel(pos_ref, invf_ref, cos_ref, sin_ref):
    pad = jnp.zeros((HEAD_DIM - ROPE_DIM, LANES), F32)
    for g in range(pos_ref.shape[0]):
        ang = pos_ref[g:g + 1, :].astype(F32) * invf_ref[...]
        for fn, out_ref in ((jnp.cos, cos_ref), (jnp.sin, sin_ref)):
            val = fn(ang)
            head = jnp.concatenate([val, val, pad], axis=0)
            out_ref[g * LANES:(g + 1) * LANES, :] = jnp.concatenate([head, head], axis=0).T


def _rope_tables(pos_rows, invf_rows):
    groups = 8
    n = pos_rows.shape[0]
    return pl.pallas_call(
        _rope_table_kernel,
        grid=(n // groups,),
        in_specs=[pl.BlockSpec((groups, LANES), lambda i: (i, 0)),
                  pl.BlockSpec((ROPE_HALF, LANES), lambda i: (0, 0))],
        out_specs=[pl.BlockSpec((groups * LANES, LANES), lambda i: (i, 0)),
                   pl.BlockSpec((groups * LANES, LANES), lambda i: (i, 0))],
        out_shape=[jax.ShapeDtypeStruct((n * LANES, LANES), F32)] * 2,
        compiler_params=_params("arbitrary"),
        name="rope_tables",
    )(pos_rows, invf_rows)


def _rope_slab(a, cos, sin, lo, hi):
    up = pltpu.roll(a, LANES - ROPE_HALF, axis=1)
    dn = pltpu.roll(a, ROPE_HALF, axis=1)
    return jnp.where(lo, a * cos - up * sin, jnp.where(hi, a * cos + dn * sin, a))


def _rope_masks(rows):
    j = lax.broadcasted_iota(jnp.int32, (rows, LANES), 1) & (HEAD_DIM - 1)
    return j < ROPE_HALF, (j >= ROPE_HALF) & (j < ROPE_DIM)


def _epilogue_q(acc, o_ref, cos_ref, sin_ref):
    cos, sin = cos_ref[...], sin_ref[...]
    lo, hi = _rope_masks(acc.shape[0])
    scale = HEAD_DIM ** -0.5 * 1.4426950408889634
    for s in range(acc.shape[1] // LANES):
        sl = slice(s * LANES, (s + 1) * LANES)
        o_ref[:, sl] = (_rope_slab(acc[:, sl], cos, sin, lo, hi) * scale).astype(o_ref.dtype)


def _epilogue_qkv(acc, o_ref, cos_ref, sin_ref):
    cos, sin = cos_ref[...], sin_ref[...]
    lo, hi = _rope_masks(acc.shape[0])
    scale = HEAD_DIM ** -0.5 * 1.4426950408889634
    for s in range(acc.shape[1] // LANES):
        sl = slice(s * LANES, (s + 1) * LANES)
        a = acc[:, sl]
        if s * LANES < Q_TAIL:
            a = _rope_slab(a, cos, sin, lo, hi) * scale
        elif s * LANES < Q_TAIL + KV_WIDTH:
            a = _rope_slab(a, cos, sin, lo, hi)
        o_ref[:, sl] = a.astype(o_ref.dtype)


def _sigmoid(x):
    return 0.5 * jnp.tanh(0.5 * x) + 0.5


def _silu(x):
    half = 0.5 * x
    return half * jnp.tanh(half) + half


def _gelu(x):
    c, a = 0.7978845608028654, 0.044715
    half = 0.5 * x
    return half * jnp.tanh(x * (x * x * (c * a) + c)) + half


def _epilogue_act(acc, o_ref, *, act):
    for s in range(acc.shape[1] // LANES):
        sl = slice(s * LANES, (s + 1) * LANES)
        o_ref[:, sl] = act(acc[:, sl]).astype(o_ref.dtype)


def _epilogue_gelu_gated(acc, o_ref, gate_ref):
    for s in range(acc.shape[1] // LANES):
        sl = slice(s * LANES, (s + 1) * LANES)
        o_ref[:, sl] = (_gelu(acc[:, sl]) * gate_ref[:, sl].astype(F32)).astype(o_ref.dtype)


def _epilogue_up_attn(acc, o_ref, sa_ref):
    for s in range(acc.shape[1] // LANES):
        sl = slice(s * LANES, (s + 1) * LANES)
        o_ref[:, sl] = (sa_ref[:, sl].astype(F32) * acc[:, sl]).astype(o_ref.dtype)


def _epilogue_up_gmlp(acc, o_ref, sb_ref, ya_ref):
    for s in range(acc.shape[1] // LANES):
        sl = slice(s * LANES, (s + 1) * LANES)
        merged = ya_ref[:, sl].astype(F32) + sb_ref[:, sl].astype(F32) * acc[:, sl]
        o_ref[:, sl] = merged.astype(o_ref.dtype)


def _matmul_kernel(a_ref, w_hbm, *rest, epilogue, n_extra, w_col, nb, mb, m_chunks, with_norm, with_side):
    if with_norm:
        g_ref, rest = rest[0], rest[1:]
    extras, rest = rest[:n_extra], rest[n_extra:]
    if with_side:
        side_in_ref, rest = rest[0], rest[1:]
    o_ref, rest = rest[0], rest[1:]
    if with_norm:
        h_ref, rest = rest[0], rest[1:]
    if with_side:
        side_out_ref, rest = rest[0], rest[1:]
        side_out_ref[...] = side_in_ref[...].astype(BF16)
    wb_ref, stage_ref, sem = rest
    n, m = pl.program_id(0), pl.program_id(1)
    k, tn = wb_ref.shape[1:]
    kc = k // mb
    cur = n % 2

    def chunk_copy(block, c, slot):
        col = pl.multiple_of(w_col(block), LANES)
        return pltpu.make_async_copy(w_hbm.at[pl.ds(c * kc, kc), pl.ds(col, tn)], stage_ref.at[slot], sem.at[slot])

    @pl.when((n == 0) & (m == 0))
    def _():
        chunk_copy(0, 0, 0).start()
        for c in range(mb):
            if c + 1 < mb:
                chunk_copy(0, c + 1, (c + 1) % 2).start()
            chunk_copy(0, c, c % 2).wait()
            wb_ref[0, c * kc:(c + 1) * kc, :] = stage_ref[c % 2].astype(BF16)

    if nb > 1:
        @pl.when(n + 1 < nb)
        def _():
            chunk_copy(n + 1, m, m % 2).start()

        @pl.when((n + 1 < nb) & (m > 0))
        def _():
            chunk_copy(n + 1, m - 1, (m - 1) % 2).wait()
            row = pl.multiple_of((m - 1) * kc, kc)
            wb_ref[1 - cur, pl.ds(row, kc), :] = stage_ref[(m - 1) % 2].astype(BF16)

        @pl.when((n > 0) & (m == 0))
        def _():
            chunk_copy(n, mb - 1, (mb - 1) % 2).wait()
            wb_ref[cur, (mb - 1) * kc:, :] = stage_ref[(mb - 1) % 2].astype(BF16)

    rows = a_ref.shape[0] // m_chunks
    for c in range(m_chunks):
        rs = pl.ds(c * rows, rows)
        lhs = a_ref[rs, :]
        if with_norm:
            ms = jnp.mean(lhs * lhs, axis=-1, keepdims=True)
            lhs = (lhs * lax.rsqrt(ms + NORM_EPS) * g_ref[...]).astype(BF16)
            h_ref[rs, :] = lhs
        acc = jnp.dot(lhs, wb_ref[cur], preferred_element_type=F32)
        epilogue(acc, o_ref.at[rs, :], *[e.at[rs, :] for e in extras])


def _matmul(name, a, w, epilogue, col_ranges, width, tm, tn, lane_tiles=(), tiles=(), m_chunks=1,
            norm_gain=None, side_convert=None):
    t, k = a.shape
    per_range = width // tn
    nb, mb = per_range * len(col_ranges), t // tm
    with_norm = norm_gain is not None
    with_side = side_convert is not None
    assert not with_norm or nb == 1

    def w_col(n):
        col = col_ranges[0]
        for r in range(1, len(col_ranges)):
            col = jnp.where(n // per_range == r, col_ranges[r], col)
        return col + n % per_range * tn

    extra_specs = [pl.BlockSpec((tm, LANES), lambda n, m: (m, 0)) for _ in lane_tiles]
    extra_specs += [pl.BlockSpec((tm, tn), functools.partial(lambda n, m, off: (m, n + off), off=off))
                    for _, off in tiles]
    extras = list(lane_tiles) + [arr for arr, _ in tiles]
    body = functools.partial(_matmul_kernel, epilogue=epilogue, n_extra=len(extras), w_col=w_col, nb=nb, mb=mb,
                             m_chunks=m_chunks, with_norm=with_norm, with_side=with_side)
    in_specs = [pl.BlockSpec((tm, k), lambda n, m: (m, 0)), pl.BlockSpec(memory_space=pl.ANY)]
    out_specs = [pl.BlockSpec((tm, tn), lambda n, m: (m, n))]
    out_shape = [jax.ShapeDtypeStruct((t, nb * tn), BF16)]
    operands = [a, w]
    if with_norm:
        in_specs.append(pl.BlockSpec((1, k), lambda n, m: (0, 0)))
        operands.append(norm_gain)
        out_specs.append(pl.BlockSpec((tm, k), lambda n, m: (m, 0)))
        out_shape.append(jax.ShapeDtypeStruct((t, k), BF16))
    in_specs += extra_specs
    operands += extras
    if with_side:
        slab = (side_convert.shape[0] // (nb * mb), side_convert.shape[1])
        in_specs.append(pl.BlockSpec(slab, lambda n, m: (n * mb + m, 0)))
        operands.append(side_convert)
        out_specs.append(pl.BlockSpec(slab, lambda n, m: (n * mb + m, 0)))
        out_shape.append(jax.ShapeDtypeStruct(side_convert.shape, BF16))
    outs = pl.pallas_call(
        body,
        grid=(nb, mb),
        in_specs=in_specs,
        out_specs=out_specs,
        out_shape=out_shape,
        scratch_shapes=[pltpu.VMEM((min(nb, 2), k, tn), BF16), pltpu.VMEM((2, k // mb, tn), F32),
                        pltpu.SemaphoreType.DMA((2,))],
        compiler_params=_params("arbitrary", "arbitrary"),
        name=name,
    )(*operands)
    return outs if len(outs) > 1 else outs[0]


def _attn_body(sink_ref, q_ref, qt_ref, kvp_ref, kvc_ref, ga_ref, o_ref, mprev_ref, mcur_ref, has_prev):
    pairs = Q_PER_KV // 2
    width = pairs * LANES
    key = lax.broadcasted_iota(jnp.int32, (BLOCK, width), 0)
    qry = lax.broadcasted_iota(jnp.int32, (BLOCK, width), 1) % LANES
    from_prev = key > qry
    left = lax.broadcasted_iota(jnp.int32, (BLOCK, LANES), 1) < HEAD_DIM
    lane_pair = lax.broadcasted_iota(jnp.int32, (1, width), 1) // LANES
    n_keys = (2 if has_prev else 1) * BLOCK
    zeros_t = jnp.zeros((HEAD_DIM, n_keys), F32)
    nt = (((1,), (1,)), ((), ()))
    log2e = 1.4426950408889634

    def exact_zero(x):
        bits = lax.bitcast_convert_type(x, jnp.uint32)
        return lax.shift_right_logical(lax.shift_right_logical(bits, jnp.uint32(16)), jnp.uint32(16))

    def head_halves(slab, e):
        swap = pltpu.roll(slab, HEAD_DIM, axis=1)
        zero = jnp.zeros_like(slab)
        if e == 0:
            return jnp.where(left, slab, zero).astype(BF16), jnp.where(left, zero, swap).astype(BF16)
        return jnp.where(left, swap, zero).astype(BF16), jnp.where(left, zero, slab).astype(BF16)

    def merged_max(s_prev, s_cur, sink):
        s = jnp.where(from_prev, s_prev if has_prev else -jnp.inf, s_cur)
        return s, jnp.maximum(jnp.max(s, axis=0, keepdims=True), sink)

    def probs(s, m):
        p = jnp.exp2(s - m).astype(BF16)
        return ([p * mprev_ref[...]] if has_prev else []) + [p * mcur_ref[...]]

    sum_row = lax.broadcasted_iota(jnp.int32, (BF16_SUBLANES, 2 * n_keys), 0)
    sum_col = lax.broadcasted_iota(jnp.int32, (BF16_SUBLANES, 2 * n_keys), 1)
    sum_rows = jnp.where((sum_row == 0) == (sum_col < n_keys), 1.0, 0.0) * (sum_row < 2)

    v_t = {}

    def scores(kv_head):
        j, e = divmod(kv_head, 2)
        kcol, vcol = j * LANES, KV_WIDTH + j * LANES
        if e == 0:
            v_rows = ([kvp_ref[:, vcol:vcol + LANES]] if has_prev else []) + [kvc_ref[:, vcol:vcol + LANES]]
            v_t[j] = jnp.concatenate(v_rows, axis=0).astype(F32).T
        kc_l, kc_r = head_halves(kvc_ref[:, kcol:kcol + LANES].astype(F32), e)
        if has_prev:
            kp_l, kp_r = head_halves(kvp_ref[:, kcol:kcol + LANES].astype(F32), e)
            k_rows = jnp.concatenate([kp_l, kc_l, kp_r, kc_r], axis=0)
        else:
            k_rows = jnp.concatenate([kc_l, kc_r], axis=0)
        slabs = [slice((kv_head * pairs + pp) * LANES, (kv_head * pairs + pp + 1) * LANES) for pp in range(pairs)]
        main = q_ref.shape[1]
        q_rows = jnp.concatenate([q_ref[:, sl] if sl.start < main else qt_ref[:, sl.start - main:sl.stop - main]
                                  for sl in slabs], axis=0)
        sink_l = jnp.zeros((1, width), F32)
        sink_r = jnp.zeros((1, width), F32)
        for pp in range(pairs):
            head = 2 * (kv_head * pairs + pp)
            sink_l = jnp.where(lane_pair == pp, sink_ref[head] * log2e, sink_l)
            sink_r = jnp.where(lane_pair == pp, sink_ref[head + 1] * log2e, sink_r)
        s = lax.dot_general(k_rows, q_rows, nt, preferred_element_type=F32)
        tiles = [s[i * BLOCK:(i + 1) * BLOCK] for i in range(s.shape[0] // BLOCK)]
        if has_prev:
            s_l, m_l = merged_max(tiles[0], tiles[1], sink_l)
            s_r, m_r = merged_max(tiles[2], tiles[3], sink_r)
        else:
            s_l, m_l = merged_max(None, tiles[0], sink_l)
            s_r, m_r = merged_max(None, tiles[1], sink_r)
        return dict(vt=v_t[j][e * HEAD_DIM:(e + 1) * HEAD_DIM], slabs=slabs, s_l=s_l, s_r=s_r, m_l=m_l, m_r=m_r,
                    sink_l=sink_l, sink_r=sink_r)

    def finish(st, after):
        vt = st["vt"]
        ones = sum_rows
        if after is not None:
            zero = exact_zero(after[0]) | exact_zero(after[1])
            ones = lax.bitcast_convert_type(lax.bitcast_convert_type(ones, jnp.uint32) + zero[:, :2 * n_keys], F32)
        v_bd = jnp.concatenate([jnp.concatenate([vt, zeros_t], axis=1),
                                jnp.concatenate([zeros_t, vt], axis=1), ones], axis=0).astype(BF16)
        p2 = jnp.concatenate(probs(st["s_l"], st["m_l"]) + probs(st["s_r"], st["m_r"]), axis=0)
        o_t = jnp.dot(v_bd, p2, preferred_element_type=F32)
        d_l = o_t[2 * HEAD_DIM:2 * HEAD_DIM + 1] + jnp.exp2(st["sink_l"] - st["m_l"])
        d_r = o_t[2 * HEAD_DIM + 1:2 * HEAD_DIM + 2] + jnp.exp2(st["sink_r"] - st["m_r"])
        o_t = jnp.concatenate([o_t[:HEAD_DIM] * (1.0 / d_l), o_t[HEAD_DIM:2 * HEAD_DIM] * (1.0 / d_r)], axis=0)
        for pp, sl in enumerate(st["slabs"]):
            o = o_t[:, pp * LANES:(pp + 1) * LANES].T
            o_ref[:, sl] = (o * ga_ref[:, sl].astype(F32)).astype(o_ref.dtype)

    st = scores(0)
    for kv_head in range(N_KV_HEADS):
        nxt = scores(kv_head + 1) if kv_head + 1 < N_KV_HEADS else None
        finish(st, (nxt["m_l"], nxt["m_r"]) if nxt is not None else None)
        st = nxt


def _attn_kernel(sink_ref, q_ref, qt_ref, kvp_ref, kvc_ref, ga_ref, o_ref, mprev_ref, mcur_ref):
    assert WINDOW == BLOCK
    blk = pl.program_id(0) % (SEQ // BLOCK)
    width = mprev_ref.shape[1]
    key = lax.broadcasted_iota(jnp.int32, (BLOCK, width), 0)
    qry = lax.broadcasted_iota(jnp.int32, (BLOCK, width), 1) % LANES
    mprev_ref[...] = (key > qry).astype(BF16)
    mcur_ref[...] = (key <= qry).astype(BF16)
    args = (sink_ref, q_ref, qt_ref, kvp_ref, kvc_ref, ga_ref, o_ref, mprev_ref, mcur_ref)
    pl.when(blk > 0)(lambda: _attn_body(*args, has_prev=True))
    pl.when(blk == 0)(lambda: _attn_body(*args, has_prev=False))


def _attention(q, qkv, act_a, sink):
    t = q.shape[0]
    nb = t // BLOCK
    assert Q_TAIL == 2 * KV_WIDTH
    return pl.pallas_call(
        _attn_kernel,
        grid_spec=pltpu.PrefetchScalarGridSpec(
            num_scalar_prefetch=1,
            grid=(nb,),
            in_specs=[pl.BlockSpec((BLOCK, ATTN_WIDTH - Q_TAIL), lambda i, s: (i, 0)),
                      pl.BlockSpec((BLOCK, Q_TAIL), lambda i, s: (i, 0)),
                      pl.BlockSpec((BLOCK, 2 * KV_WIDTH), lambda i, s: (jnp.maximum(i - 1, 0), 1)),
                      pl.BlockSpec((BLOCK, 2 * KV_WIDTH), lambda i, s: (i, 1)),
                      pl.BlockSpec((BLOCK, ATTN_WIDTH), lambda i, s: (i, 0))],
            out_specs=pl.BlockSpec((BLOCK, ATTN_WIDTH), lambda i, s: (i, 0)),
            scratch_shapes=[pltpu.VMEM((BLOCK, (Q_PER_KV // 2) * LANES), BF16)] * 2,
        ),
        out_shape=jax.ShapeDtypeStruct((t, ATTN_WIDTH), BF16),
        compiler_params=_params("arbitrary"),
        name="swa_sink_attention",
    )(sink, q, qkv, qkv, qkv, act_a)


def _gating_kernel(u_ref, v_ref, lng_ref, lnb_ref, ws_ref, bt_ref, o_ref):
    v = v_ref[...].astype(F32)
    mu = jnp.mean(v, axis=-1, keepdims=True)
    vc = v - mu
    var = jnp.mean(vc * vc, axis=-1, keepdims=True)
    vn = (vc * lax.rsqrt(var + LN_EPS) * lng_ref[...] + lnb_ref[...]).astype(BF16)
    ti = lax.broadcasted_iota(jnp.int32, (GMLP_CHUNK, GMLP_CHUNK), 0)
    si = lax.broadcasted_iota(jnp.int32, (GMLP_CHUNK, GMLP_CHUNK), 1)
    causal = si <= ti
    bt = bt_ref[...]
    for g in range(GMLP_GROUPS):
        w = jnp.where(causal, ws_ref[g], 0.0).astype(BF16)
        bias = bt[:, g:g + 1]
        cols = slice(g * GMLP_GROUP_DIM, (g + 1) * GMLP_GROUP_DIM)
        for c in range(v.shape[0] // GMLP_CHUNK):
            rows = slice(c * GMLP_CHUNK, (c + 1) * GMLP_CHUNK)
            mixed = jnp.dot(w, vn[rows, cols], preferred_element_type=F32) + bias
            o_ref[rows, cols] = (u_ref[rows, cols].astype(F32) * mixed).astype(o_ref.dtype)


def _spatial_gating(act_u, act_v, ln_g, ln_b, w_s, b_t):
    t = act_u.shape[0]
    r = 2 * GMLP_CHUNK
    w = GMLP_WIDTH
    return pl.pallas_call(
        _gating_kernel,
        grid=(t // r,),
        in_specs=[pl.BlockSpec((r, w), lambda i: (i, 0)),
                  pl.BlockSpec((r, w), lambda i: (i, 0)),
                  pl.BlockSpec((1, w), lambda i: (0, 0)),
                  pl.BlockSpec((1, w), lambda i: (0, 0)),
                  pl.BlockSpec((GMLP_GROUPS, GMLP_CHUNK, GMLP_CHUNK), lambda i: (0, 0, 0)),
                  pl.BlockSpec((GMLP_CHUNK, GMLP_GROUPS), lambda i: (0, 0))],
        out_specs=pl.BlockSpec((r, w), lambda i: (i, 0)),
        out_shape=jax.ShapeDtypeStruct((t, w), BF16),
        compiler_params=_params("arbitrary"),
        name="spatial_gating",
    )(act_u, act_v, ln_g, ln_b, w_s, b_t)


def _out_kernel(a_ref, w_ref, x_ref, g_ref, o_ref, y_ref, ss_ref, *, n_row_blocks):
    i, j = pl.program_id(0), pl.program_id(1)
    tn = x_ref.shape[1]
    col = pl.multiple_of(j * tn, tn)
    cur = i % 2

    @pl.when(i > 0)
    def _():
        ms_prev = jnp.sum(ss_ref[1 - cur], axis=-1, keepdims=True) * (1.0 / D_MODEL)
        o_ref[...] = y_ref[:, pl.ds(col, tn)] * lax.rsqrt(ms_prev + NORM_EPS) * g_ref[:, pl.ds(col, tn)]

    @pl.when(i < n_row_blocks)
    def _():
        y = x_ref[...] + jnp.dot(a_ref[...], w_ref[...], preferred_element_type=F32)
        sq = y * y
        part = sq[:, :LANES]
        for s in range(1, tn // LANES):
            part = part + sq[:, s * LANES:(s + 1) * LANES]
        ss_ref[cur] = jnp.where(j == 0, part, ss_ref[cur] + part)
        y_ref[:, pl.ds(col, tn)] = y


def _out_proj(a, w, x2, g_row):
    t, k = a.shape
    n = w.shape[1]
    last = t // TM_OUT - 1
    return pl.pallas_call(
        functools.partial(_out_kernel, n_row_blocks=last + 1),
        grid=(t // TM_OUT + 1, n // TN_OUT),
        in_specs=[pl.BlockSpec((TM_OUT, k), lambda i, j: (jnp.minimum(i, last), 0)),
                  pl.BlockSpec((k, TN_OUT), lambda i, j: (0, j)),
                  pl.BlockSpec((TM_OUT, TN_OUT), lambda i, j: (jnp.minimum(i, last), j)),
                  pl.BlockSpec((1, n), lambda i, j: (0, 0))],
        out_specs=pl.BlockSpec((TM_OUT, TN_OUT), lambda i, j: (jnp.maximum(i - 1, 0), jnp.where(i == 0, 0, j))),
        out_shape=jax.ShapeDtypeStruct((t, n), F32),
        scratch_shapes=[pltpu.VMEM((TM_OUT, n), F32), pltpu.VMEM((2, TM_OUT, LANES), F32)],
        compiler_params=_params("arbitrary", "arbitrary"),
        name="out_proj_residual_norm",
    )(a, w, x2, g_row)


def _rope_inv_freq_rows():
    inv_freq = ROPE_THETA ** (-jnp.arange(ROPE_HALF, dtype=F32) * 2.0 / ROPE_DIM)
    return jnp.broadcast_to(inv_freq[:, None], (ROPE_HALF, LANES))


def kernel(x, positions, norm_g, w_in, attn_sink, gmlp_ln_g, gmlp_ln_b, w_spatial, b_spatial, w_up_attn,
           w_up_gmlp, w_out, final_norm_g):
    b, s, d = x.shape
    t = b * s
    assert s == SEQ and d == D_MODEL and norm_g.shape[0] == DEPTH == 1
    x2 = x.reshape(t, d)
    cos_t, sin_t = _rope_tables(positions.reshape(t // LANES, LANES), _rope_inv_freq_rows())
    nw = D_MODEL // TN
    w_i = w_in[0]
    qkv_w = Q_TAIL + 2 * KV_WIDTH
    qkv, h = _matmul("norm_in_proj_qkv", x2, w_i, _epilogue_qkv, [OFF_KV - Q_TAIL], qkv_w, TM_NORM, qkv_w,
                     lane_tiles=(cos_t, sin_t), m_chunks=NORM_CHUNKS, norm_gain=norm_g)
    q = _matmul("in_proj_q", h, w_i, _epilogue_q, [OFF_Q], ATTN_WIDTH - Q_TAIL, TM, TN, lane_tiles=(cos_t, sin_t),
                m_chunks=Q_CHUNKS)
    act_gate = _matmul("in_proj_silu", h, w_i, functools.partial(_epilogue_act, act=_silu),
                       [OFF_GATE_A, OFF_GATE_B], ATTN_WIDTH, TM, TN)
    act_u = _matmul("in_proj_gelu_u", h, w_i, _epilogue_gelu_gated, [OFF_U], GMLP_WIDTH, TM, TN,
                    tiles=[(act_gate, nw)])
    act_v = _matmul("in_proj_gelu_v", h, w_i, functools.partial(_epilogue_act, act=_gelu),
                    [OFF_U + GMLP_WIDTH], GMLP_WIDTH, TM, TN)
    act_merge = _matmul("in_proj_sigmoid", h, w_i, functools.partial(_epilogue_act, act=_sigmoid),
                        [OFF_MERGE], 2 * D_MODEL, TM, TN)
    attn_g = _attention(q, qkv, act_gate, attn_sink[0])
    ya, w_out_b = _matmul("up_attn", attn_g, w_up_attn[0], _epilogue_up_attn, [0], D_MODEL, TM, TN,
                          tiles=[(act_merge, 0)], side_convert=w_out[0])
    sgb = _spatial_gating(act_u, act_v, gmlp_ln_g, gmlp_ln_b, w_spatial[0], b_spatial[0].T)
    merged = _matmul("up_gmlp_merge", sgb, w_up_gmlp[0], _epilogue_up_gmlp, [0], D_MODEL, TM, TN,
                     tiles=[(act_merge, nw), (ya, 0)])
    out = _out_proj(merged, w_out_b, x2, final_norm_g[None, :])
    return out.reshape(b, s, d)
```

```python
import functools

import jax
import jax.numpy as jnp
from jax import lax
from jax.experimental import pallas as pl
from jax.experimental.pallas import tpu as pltpu

D_MODEL = 4096
SEQ = 4096
DEPTH = 1
N_Q_HEADS = 64
N_KV_HEADS = 8
HEAD_DIM = 64
Q_PER_KV = N_Q_HEADS // N_KV_HEADS
ATTN_WIDTH = N_Q_HEADS * HEAD_DIM
KV_WIDTH = N_KV_HEADS * HEAD_DIM
WINDOW = 128
BLOCK = 128
ROPE_THETA = 500000.0
ROPE_DIM = HEAD_DIM // 4
ROPE_HALF = ROPE_DIM // 2
GMLP_WIDTH = D_MODEL
GMLP_GROUPS = 8
GMLP_GROUP_DIM = GMLP_WIDTH // GMLP_GROUPS
GMLP_CHUNK = 128
NORM_EPS = 1e-5
LN_EPS = 1e-5

OFF_Q = 0
OFF_KV = ATTN_WIDTH
OFF_GATE_A = OFF_KV + 2 * KV_WIDTH
OFF_U = OFF_GATE_A + ATTN_WIDTH
OFF_GATE_B = OFF_U + 2 * GMLP_WIDTH
OFF_MERGE = OFF_GATE_B + GMLP_WIDTH

LANES = 128
BF16_SUBLANES = 16
VMEM_LIMIT = 56 * 1024 * 1024

TM = 1024
TN = 1024
Q_CHUNKS = 4
TM_NORM = 512
Q_TAIL = 1024
NORM_CHUNKS = 2
ATTN_BLOCKS = 4
TM_OUT = 1024
TN_OUT = 512

BF16 = jnp.bfloat16
F32 = jnp.float32


def _params(*sem):
    return pltpu.CompilerParams(dimension_semantics=sem, vmem_limit_bytes=VMEM_LIMIT)


def _rope_table_kernel(pos_ref, invf_ref, cos_ref, sin_ref):
    pad = jnp.zeros((HEAD_DIM - ROPE_DIM, LANES), F32)
    for g in range(pos_ref.shape[0]):
        ang = pos_ref[g:g + 1, :].astype(F32) * invf_ref[...]
        for fn, out_ref in ((jnp.cos, cos_ref), (jnp.sin, sin_ref)):
            val = fn(ang)
            head = jnp.concatenate([val, val, pad], axis=0)
            out_ref[g * LANES:(g + 1) * LANES, :] = jnp.concatenate([head, head], axis=0).T


def _rope_tables(pos_rows, invf_rows):
    groups = 8
    n = pos_rows.shape[0]
    return pl.pallas_call(
        _rope_table_kernel,
        grid=(n // groups,),
        in_specs=[pl.BlockSpec((groups, LANES), lambda i: (i, 0)),
                  pl.BlockSpec((ROPE_HALF, LANES), lambda i: (0, 0))],
        out_specs=[pl.BlockSpec((groups * LANES, LANES), lambda i: (i, 0)),
                   pl.BlockSpec((groups * LANES, LANES), lambda i: (i, 0))],
        out_shape=[jax.ShapeDtypeStruct((n * LANES, LANES), F32)] * 2,
        compiler_params=_params("arbitrary"),
        name="rope_tables",
    )(pos_rows, invf_rows)


def _rope_slab(a, cos, sin, lo, hi):
    up = pltpu.roll(a, LANES - ROPE_HALF, axis=1)
    dn = pltpu.roll(a, ROPE_HALF, axis=1)
    return jnp.where(lo, a * cos - up * sin, jnp.where(hi, a * cos + dn * sin, a))


def _rope_masks(rows):
    j = lax.broadcasted_iota(jnp.int32, (rows, LANES), 1) & (HEAD_DIM - 1)
    return j < ROPE_HALF, (j >= ROPE_HALF) & (j < ROPE_DIM)


def _epilogue_q(acc, o_ref, cos_ref, sin_ref):
    cos, sin = cos_ref[...], sin_ref[...]
    lo, hi = _rope_masks(acc.shape[0])
    scale = HEAD_DIM ** -0.5 * 1.4426950408889634
    for s in range(acc.shape[1] // LANES):
        sl = slice(s * LANES, (s + 1) * LANES)
        o_ref[:, sl] = (_rope_slab(acc[:, sl], cos, sin, lo, hi) * scale).astype(o_ref.dtype)


def _epilogue_qkv(acc, o_ref, cos_ref, sin_ref):
    cos, sin = cos_ref[...], sin_ref[...]
    lo, hi = _rope_masks(acc.shape[0])
    scale = HEAD_DIM ** -0.5 * 1.4426950408889634
    for s in range(acc.shape[1] // LANES):
        sl = slice(s * LANES, (s + 1) * LANES)
        a = acc[:, sl]
        if s * LANES < Q_TAIL:
            a = _rope_slab(a, cos, sin, lo, hi) * scale
        elif s * LANES < Q_TAIL + KV_WIDTH:
            a = _rope_slab(a, cos, sin, lo, hi)
        o_ref[:, sl] = a.astype(o_ref.dtype)


def _sigmoid(x):
    return 0.5 * jnp.tanh(0.5 * x) + 0.5


def _silu(x):
    half = 0.5 * x
    return half * jnp.tanh(half) + half


def _gelu(x):
    c, a = 0.7978845608028654, 0.044715
    half = 0.5 * x
    return half * jnp.tanh(x * (x * x * (c * a) + c)) + half


def _epilogue_act(acc, o_ref, *, act):
    for s in range(acc.shape[1] // LANES):
        sl = slice(s * LANES, (s + 1) * LANES)
        o_ref[:, sl] = act(acc[:, sl]).astype(o_ref.dtype)


def _epilogue_gelu_gated(acc, o_ref, gate_ref):
    for s in range(acc.shape[1] // LANES):
        sl = slice(s * LANES, (s + 1) * LANES)
        o_ref[:, sl] = (_gelu(acc[:, sl]) * gate_ref[:, sl].astype(F32)).astype(o_ref.dtype)


def _epilogue_up_attn(acc, o_ref, sa_ref):
    for s in range(acc.shape[1] // LANES):
        sl = slice(s * LANES, (s + 1) * LANES)
        o_ref[:, sl] = (sa_ref[:, sl].astype(F32) * acc[:, sl]).astype(o_ref.dtype)


def _epilogue_up_gmlp(acc, o_ref, sb_ref, ya_ref):
    for s in range(acc.shape[1] // LANES):
        sl = slice(s * LANES, (s + 1) * LANES)
        merged = ya_ref[:, sl].astype(F32) + sb_ref[:, sl].astype(F32) * acc[:, sl]
        o_ref[:, sl] = merged.astype(o_ref.dtype)


def _matmul_kernel(a_ref, w_hbm, *rest, epilogue, n_extra, w_col, nb, mb, m_chunks, with_norm, with_side):
    if with_norm:
        g_ref, rest = rest[0], rest[1:]
    extras, rest = rest[:n_extra], rest[n_extra:]
    if with_side:
        side_in_ref, rest = rest[0], rest[1:]
    o_ref, rest = rest[0], rest[1:]
    if with_norm:
        h_ref, rest = rest[0], rest[1:]
    if with_side:
        side_out_ref, rest = rest[0], rest[1:]
        side_out_ref[...] = side_in_ref[...].astype(BF16)
    wb_ref, stage_ref, sem = rest
    n, m = pl.program_id(0), pl.program_id(1)
    k, tn = wb_ref.shape[1:]
    kc = k // mb
    cur = n % 2

    def chunk_copy(block, c, slot):
        col = pl.multiple_of(w_col(block), LANES)
        return pltpu.make_async_copy(w_hbm.at[pl.ds(c * kc, kc), pl.ds(col, tn)], stage_ref.at[slot], sem.at[slot])

    @pl.when((n == 0) & (m == 0))
    def _():
        chunk_copy(0, 0, 0).start()
        for c in range(mb):
            if c + 1 < mb:
                chunk_copy(0, c + 1, (c + 1) % 2).start()
            chunk_copy(0, c, c % 2).wait()
            wb_ref[0, c * kc:(c + 1) * kc, :] = stage_ref[c % 2].astype(BF16)

    if nb > 1:
        @pl.when(n + 1 < nb)
        def _():
            chunk_copy(n + 1, m, m % 2).start()

        @pl.when((n + 1 < nb) & (m > 0))
        def _():
            chunk_copy(n + 1, m - 1, (m - 1) % 2).wait()
            row = pl.multiple_of((m - 1) * kc, kc)
            wb_ref[1 - cur, pl.ds(row, kc), :] = stage_ref[(m - 1) % 2].astype(BF16)

        @pl.when((n > 0) & (m == 0))
        def _():
            chunk_copy(n, mb - 1, (mb - 1) % 2).wait()
            wb_ref[cur, (mb - 1) * kc:, :] = stage_ref[(mb - 1) % 2].astype(BF16)

    rows = a_ref.shape[0] // m_chunks
    for c in range(m_chunks):
        rs = pl.ds(c * rows, rows)
        lhs = a_ref[rs, :]
        if with_norm:
            ms = jnp.mean(lhs * lhs, axis=-1, keepdims=True)
            lhs = (lhs * lax.rsqrt(ms + NORM_EPS) * g_ref[...]).astype(BF16)
            h_ref[rs, :] = lhs
        acc = jnp.dot(lhs, wb_ref[cur], preferred_element_type=F32)
        epilogue(acc, o_ref.at[rs, :], *[e.at[rs, :] for e in extras])


def _matmul(name, a, w, epilogue, col_ranges, width, tm, tn, lane_tiles=(), tiles=(), m_chunks=1,
            norm_gain=None, side_convert=None):
    t, k = a.shape
    per_range = width // tn
    nb, mb = per_range * len(col_ranges), t // tm
    with_norm = norm_gain is not None
    with_side = side_convert is not None
    assert not with_norm or nb == 1

    def w_col(n):
        col = col_ranges[0]
        for r in range(1, len(col_ranges)):
            col = jnp.where(n // per_range == r, col_ranges[r], col)
        return col + n % per_range * tn

    extra_specs = [pl.BlockSpec((tm, LANES), lambda n, m: (m, 0)) for _ in lane_tiles]
    extra_specs += [pl.BlockSpec((tm, tn), functools.partial(lambda n, m, off: (m, n + off), off=off))
                    for _, off in tiles]
    extras = list(lane_tiles) + [arr for arr, _ in tiles]
    body = functools.partial(_matmul_kernel, epilogue=epilogue, n_extra=len(extras), w_col=w_col, nb=nb, mb=mb,
                             m_chunks=m_chunks, with_norm=with_norm, with_side=with_side)
    in_specs = [pl.BlockSpec((tm, k), lambda n, m: (m, 0)), pl.BlockSpec(memory_space=pl.ANY)]
    out_specs = [pl.BlockSpec((tm, tn), lambda n, m: (m, n))]
    out_shape = [jax.ShapeDtypeStruct((t, nb * tn), BF16)]
    operands = [a, w]
    if with_norm:
        in_specs.append(pl.BlockSpec((1, k), lambda n, m: (0, 0)))
        operands.append(norm_gain)
        out_specs.append(pl.BlockSpec((tm, k), lambda n, m: (m, 0)))
        out_shape.append(jax.ShapeDtypeStruct((t, k), BF16))
    in_specs += extra_specs
    operands += extras
    if with_side:
        slab = (side_convert.shape[0] // (nb * mb), side_convert.shape[1])
        in_specs.append(pl.BlockSpec(slab, lambda n, m: (n * mb + m, 0)))
        operands.append(side_convert)
        out_specs.append(pl.BlockSpec(slab, lambda n, m: (n * mb + m, 0)))
        out_shape.append(jax.ShapeDtypeStruct(side_convert.shape, BF16))
    outs = pl.pallas_call(
        body,
        grid=(nb, mb),
        in_specs=in_specs,
        out_specs=out_specs,
        out_shape=out_shape,
        scratch_shapes=[pltpu.VMEM((min(nb, 2), k, tn), BF16), pltpu.VMEM((2, k // mb, tn), F32),
                        pltpu.SemaphoreType.DMA((2,))],
        compiler_params=_params("arbitrary", "arbitrary"),
        name=name,
    )(*operands)
    return outs if len(outs) > 1 else outs[0]


def _attn_body(sink_ref, q_ref, qt_ref, kvp_ref, kvc_ref, ga_ref, o_ref, mprev_ref, mcur_ref, has_prev):
    pairs = Q_PER_KV // 2
    width = pairs * LANES
    key = lax.broadcasted_iota(jnp.int32, (BLOCK, width), 0)
    qry = lax.broadcasted_iota(jnp.int32, (BLOCK, width), 1) % LANES
    from_prev = key > qry
    left = lax.broadcasted_iota(jnp.int32, (BLOCK, LANES), 1) < HEAD_DIM
    lane_pair = lax.broadcasted_iota(jnp.int32, (1, width), 1) // LANES
    n_keys = (2 if has_prev else 1) * BLOCK
    zeros_t = jnp.zeros((HEAD_DIM, n_keys), F32)
    nt = (((1,), (1,)), ((), ()))
    log2e = 1.4426950408889634

    def exact_zero(x):
        bits = lax.bitcast_convert_type(x, jnp.uint32)
        return lax.shift_right_logical(lax.shift_right_logical(bits, jnp.uint32(16)), jnp.uint32(16))

    def head_halves(slab, e):
        swap = pltpu.roll(slab, HEAD_DIM, axis=1)
        zero = jnp.zeros_like(slab)
        if e == 0:
            return jnp.where(left, slab, zero).astype(BF16), jnp.where(left, zero, swap).astype(BF16)
        return jnp.where(left, swap, zero).astype(BF16), jnp.where(left, zero, slab).astype(BF16)

    def merged_max(s_prev, s_cur, sink):
        s = jnp.where(from_prev, s_prev if has_prev else -jnp.inf, s_cur)
        return s, jnp.maximum(jnp.max(s, axis=0, keepdims=True), sink)

    def probs(s, m):
        p = jnp.exp2(s - m).astype(BF16)
        return ([p * mprev_ref[...]] if has_prev else []) + [p * mcur_ref[...]]

    sum_row = lax.broadcasted_iota(jnp.int32, (BF16_SUBLANES, 2 * n_keys), 0)
    sum_col = lax.broadcasted_iota(jnp.int32, (BF16_SUBLANES, 2 * n_keys), 1)
    sum_rows = jnp.where((sum_row == 0) == (sum_col < n_keys), 1.0, 0.0) * (sum_row < 2)

    v_t = {}

    def scores(kv_head):
        j, e = divmod(kv_head, 2)
        kcol, vcol = j * LANES, KV_WIDTH + j * LANES
        if e == 0:
            v_rows = ([kvp_ref[:, vcol:vcol + LANES]] if has_prev else []) + [kvc_ref[:, vcol:vcol + LANES]]
            v_t[j] = jnp.concatenate(v_rows, axis=0).astype(F32).T
        kc_l, kc_r = head_halves(kvc_ref[:, kcol:kcol + LANES].astype(F32), e)
        if has_prev:
            kp_l, kp_r = head_halves(kvp_ref[:, kcol:kcol + LANES].astype(F32), e)
            k_rows = jnp.concatenate([kp_l, kc_l, kp_r, kc_r], axis=0)
        else:
            k_rows = jnp.concatenate([kc_l, kc_r], axis=0)
        slabs = [slice((kv_head * pairs + pp) * LANES, (kv_head * pairs + pp + 1) * LANES) for pp in range(pairs)]
        main = q_ref.shape[1]
        q_rows = jnp.concatenate([q_ref[:, sl] if sl.start < main else qt_ref[:, sl.start - main:sl.stop - main]
                                  for sl in slabs], axis=0)
        sink_l = jnp.zeros((1, width), F32)
        sink_r = jnp.zeros((1, width), F32)
        for pp in range(pairs):
            head = 2 * (kv_head * pairs + pp)
            sink_l = jnp.where(lane_pair == pp, sink_ref[head] * log2e, sink_l)
            sink_r = jnp.where(lane_pair == pp, sink_ref[head + 1] * log2e, sink_r)
        s = lax.dot_general(k_rows, q_rows, nt, preferred_element_type=F32)
        tiles = [s[i * BLOCK:(i + 1) * BLOCK] for i in range(s.shape[0] // BLOCK)]
        if has_prev:
            s_l, m_l = merged_max(tiles[0], tiles[1], sink_l)
            s_r, m_r = merged_max(tiles[2], tiles[3], sink_r)
        else:
            s_l, m_l = merged_max(None, tiles[0], sink_l)
            s_r, m_r = merged_max(None, tiles[1], sink_r)
        return dict(vt=v_t[j][e * HEAD_DIM:(e + 1) * HEAD_DIM], slabs=slabs, s_l=s_l, s_r=s_r, m_l=m_l, m_r=m_r,
                    sink_l=sink_l, sink_r=sink_r)

    def finish(st, after):
        vt = st["vt"]
        ones = sum_rows
        if after is not None:
            zero = exact_zero(after[0]) | exact_zero(after[1])
            ones = lax.bitcast_convert_type(lax.bitcast_convert_type(ones, jnp.uint32) + zero[:, :2 * n_keys], F32)
        v_bd = jnp.concatenate([jnp.concatenate([vt, zeros_t], axis=1),
                                jnp.concatenate([zeros_t, vt], axis=1), ones], axis=0).astype(BF16)
        p2 = jnp.concatenate(probs(st["s_l"], st["m_l"]) + probs(st["s_r"], st["m_r"]), axis=0)
        o_t = jnp.dot(v_bd, p2, preferred_element_type=F32)
        d_l = o_t[2 * HEAD_DIM:2 * HEAD_DIM + 1] + jnp.exp2(st["sink_l"] - st["m_l"])
        d_r = o_t[2 * HEAD_DIM + 1:2 * HEAD_DIM + 2] + jnp.exp2(st["sink_r"] - st["m_r"])
        o_t = jnp.concatenate([o_t[:HEAD_DIM] * (1.0 / d_l), o_t[HEAD_DIM:2 * HEAD_DIM] * (1.0 / d_r)], axis=0)
        for pp, sl in enumerate(st["slabs"]):
            o = o_t[:, pp * LANES:(pp + 1) * LANES].T
            o_ref[:, sl] = (o * ga_ref[:, sl].astype(F32)).astype(o_ref.dtype)

    st = scores(0)
    for kv_head in range(N_KV_HEADS):
        nxt = scores(kv_head + 1) if kv_head + 1 < N_KV_HEADS else None
        finish(st, (nxt["m_l"], nxt["m_r"]) if nxt is not None else None)
        st = nxt


def _attn_kernel(sink_ref, q_ref, qt_ref, kvp_ref, kvc_ref, ga_ref, o_ref, mprev_ref, mcur_ref):
    assert WINDOW == BLOCK
    blk = pl.program_id(0) % (SEQ // (ATTN_BLOCKS * BLOCK))
    width = mprev_ref.shape[1]
    key = lax.broadcasted_iota(jnp.int32, (BLOCK, width), 0)
    qry = lax.broadcasted_iota(jnp.int32, (BLOCK, width), 1) % LANES
    mprev_ref[...] = (key > qry).astype(BF16)
    mcur_ref[...] = (key <= qry).astype(BF16)
    def block(b, kv_prev, has_prev):
        rows = pl.ds(b * BLOCK, BLOCK)
        _attn_body(sink_ref, q_ref.at[rows, :], qt_ref.at[rows, :], kv_prev, kvc_ref.at[rows, :], ga_ref.at[rows, :],
                   o_ref.at[rows, :], mprev_ref, mcur_ref, has_prev=has_prev)

    pl.when(blk > 0)(lambda: block(0, kvp_ref, True))
    pl.when(blk == 0)(lambda: block(0, kvp_ref, False))
    for b in range(1, ATTN_BLOCKS):
        block(b, kvc_ref.at[pl.ds((b - 1) * BLOCK, BLOCK), :], True)


def _attention(q, qkv, act_a, sink):
    t = q.shape[0]
    rows = ATTN_BLOCKS * BLOCK
    assert Q_TAIL == 2 * KV_WIDTH
    return pl.pallas_call(
        _attn_kernel,
        grid_spec=pltpu.PrefetchScalarGridSpec(
            num_scalar_prefetch=1,
            grid=(t // rows,),
            in_specs=[pl.BlockSpec((rows, ATTN_WIDTH - Q_TAIL), lambda i, s: (i, 0)),
                      pl.BlockSpec((rows, Q_TAIL), lambda i, s: (i, 0)),
                      pl.BlockSpec((BLOCK, 2 * KV_WIDTH), lambda i, s: (jnp.maximum(i * ATTN_BLOCKS - 1, 0), 1)),
                      pl.BlockSpec((rows, 2 * KV_WIDTH), lambda i, s: (i, 1)),
                      pl.BlockSpec((rows, ATTN_WIDTH), lambda i, s: (i, 0))],
            out_specs=pl.BlockSpec((rows, ATTN_WIDTH), lambda i, s: (i, 0)),
            scratch_shapes=[pltpu.VMEM((BLOCK, (Q_PER_KV // 2) * LANES), BF16)] * 2,
        ),
        out_shape=jax.ShapeDtypeStruct((t, ATTN_WIDTH), BF16),
        compiler_params=_params("arbitrary"),
        name="swa_sink_attention",
    )(sink, q, qkv, qkv, qkv, act_a)


def _gating_kernel(u_ref, v_ref, lng_ref, lnb_ref, ws_ref, bt_ref, o_ref):
    v = v_ref[...].astype(F32)
    mu = jnp.mean(v, axis=-1, keepdims=True)
    vc = v - mu
    var = jnp.mean(vc * vc, axis=-1, keepdims=True)
    vn = (vc * lax.rsqrt(var + LN_EPS) * lng_ref[...] + lnb_ref[...]).astype(BF16)
    ti = lax.broadcasted_iota(jnp.int32, (GMLP_CHUNK, GMLP_CHUNK), 0)
    si = lax.broadcasted_iota(jnp.int32, (GMLP_CHUNK, GMLP_CHUNK), 1)
    causal = si <= ti
    bt = bt_ref[...]
    for g in range(GMLP_GROUPS):
        w = jnp.where(causal, ws_ref[g], 0.0).astype(BF16)
        bias = bt[:, g:g + 1]
        cols = slice(g * GMLP_GROUP_DIM, (g + 1) * GMLP_GROUP_DIM)
        for c in range(v.shape[0] // GMLP_CHUNK):
            rows = slice(c * GMLP_CHUNK, (c + 1) * GMLP_CHUNK)
            mixed = jnp.dot(w, vn[rows, cols], preferred_element_type=F32) + bias
            o_ref[rows, cols] = (u_ref[rows, cols].astype(F32) * mixed).astype(o_ref.dtype)


def _spatial_gating(act_u, act_v, ln_g, ln_b, w_s, b_t):
    t = act_u.shape[0]
    r = 4 * GMLP_CHUNK
    w = GMLP_WIDTH
    return pl.pallas_call(
        _gating_kernel,
        grid=(t // r,),
        in_specs=[pl.BlockSpec((r, w), lambda i: (i, 0)),
                  pl.BlockSpec((r, w), lambda i: (i, 0)),
                  pl.BlockSpec((1, w), lambda i: (0, 0)),
                  pl.BlockSpec((1, w), lambda i: (0, 0)),
                  pl.BlockSpec((GMLP_GROUPS, GMLP_CHUNK, GMLP_CHUNK), lambda i: (0, 0, 0)),
                  pl.BlockSpec((GMLP_CHUNK, GMLP_GROUPS), lambda i: (0, 0))],
        out_specs=pl.BlockSpec((r, w), lambda i: (i, 0)),
        out_shape=jax.ShapeDtypeStruct((t, w), BF16),
        compiler_params=_params("arbitrary"),
        name="spatial_gating",
    )(act_u, act_v, ln_g, ln_b, w_s, b_t)


def _out_kernel(a_ref, w_ref, x_ref, g_ref, o_ref, y_ref, ss_ref, *, n_row_blocks):
    i, j = pl.program_id(0), pl.program_id(1)
    tn = x_ref.shape[1]
    col = pl.multiple_of(j * tn, tn)
    cur = i % 2

    @pl.when(i > 0)
    def _():
        ms_prev = jnp.sum(ss_ref[1 - cur], axis=-1, keepdims=True) * (1.0 / D_MODEL)
        o_ref[...] = y_ref[:, pl.ds(col, tn)] * lax.rsqrt(ms_prev + NORM_EPS) * g_ref[:, pl.ds(col, tn)]

    @pl.when(i < n_row_blocks)
    def _():
        y = x_ref[...] + jnp.dot(a_ref[...], w_ref[...], preferred_element_type=F32)
        sq = y * y
        part = sq[:, :LANES]
        for s in range(1, tn // LANES):
            part = part + sq[:, s * LANES:(s + 1) * LANES]
        ss_ref[cur] = jnp.where(j == 0, part, ss_ref[cur] + part)
        y_ref[:, pl.ds(col, tn)] = y


def _out_proj(a, w, x2, g_row):
    t, k = a.shape
    n = w.shape[1]
    last = t // TM_OUT - 1
    return pl.pallas_call(
        functools.partial(_out_kernel, n_row_blocks=last + 1),
        grid=(t // TM_OUT + 1, n // TN_OUT),
        in_specs=[pl.BlockSpec((TM_OUT, k), lambda i, j: (jnp.minimum(i, last), 0)),
                  pl.BlockSpec((k, TN_OUT), lambda i, j: (0, j)),
                  pl.BlockSpec((TM_OUT, TN_OUT), lambda i, j: (jnp.minimum(i, last), j)),
                  pl.BlockSpec((1, n), lambda i, j: (0, 0))],
        out_specs=pl.BlockSpec((TM_OUT, TN_OUT), lambda i, j: (jnp.maximum(i - 1, 0), jnp.where(i == 0, 0, j))),
        out_shape=jax.ShapeDtypeStruct((t, n), F32),
        scratch_shapes=[pltpu.VMEM((TM_OUT, n), F32), pltpu.VMEM((2, TM_OUT, LANES), F32)],
        compiler_params=_params("arbitrary", "arbitrary"),
        name="out_proj_residual_norm",
    )(a, w, x2, g_row)


def _rope_inv_freq_rows():
    inv_freq = ROPE_THETA ** (-jnp.arange(ROPE_HALF, dtype=F32) * 2.0 / ROPE_DIM)
    return jnp.broadcast_to(inv_freq[:, None], (ROPE_HALF, LANES))


def kernel(x, positions, norm_g, w_in, attn_sink, gmlp_ln_g, gmlp_ln_b, w_spatial, b_spatial, w_up_attn,
           w_up_gmlp, w_out, final_norm_g):
    b, s, d = x.shape
    t = b * s
    assert s == SEQ and d == D_MODEL and norm_g.shape[0] == DEPTH == 1
    x2 = x.reshape(t, d)
    cos_t, sin_t = _rope_tables(positions.reshape(t // LANES, LANES), _rope_inv_freq_rows())
    nw = D_MODEL // TN
    w_i = w_in[0]
    qkv_w = Q_TAIL + 2 * KV_WIDTH
    qkv, h = _matmul("norm_in_proj_qkv", x2, w_i, _epilogue_qkv, [OFF_KV - Q_TAIL], qkv_w, TM_NORM, qkv_w,
                     lane_tiles=(cos_t, sin_t), m_chunks=NORM_CHUNKS, norm_gain=norm_g)
    q = _matmul("in_proj_q", h, w_i, _epilogue_q, [OFF_Q], ATTN_WIDTH - Q_TAIL, TM, TN, lane_tiles=(cos_t, sin_t),
                m_chunks=Q_CHUNKS)
    act_gate = _matmul("in_proj_silu", h, w_i, functools.partial(_epilogue_act, act=_silu),
                       [OFF_GATE_A, OFF_GATE_B], ATTN_WIDTH, TM, TN)
    act_u = _matmul("in_proj_gelu_u", h, w_i, _epilogue_gelu_gated, [OFF_U], GMLP_WIDTH, TM, TN,
                    tiles=[(act_gate, nw)])
    act_v = _matmul("in_proj_gelu_v", h, w_i, functools.partial(_epilogue_act, act=_gelu),
                    [OFF_U + GMLP_WIDTH], GMLP_WIDTH, TM, TN)
    act_merge = _matmul("in_proj_sigmoid", h, w_i, functools.partial(_epilogue_act, act=_sigmoid),
                        [OFF_MERGE], 2 * D_MODEL, TM // 2, 2 * TN)
    attn_g = _attention(q, qkv, act_gate, attn_sink[0])
    ya, w_out_b = _matmul("up_attn", attn_g, w_up_attn[0], _epilogue_up_attn, [0], D_MODEL, TM, TN,
                          tiles=[(act_merge, 0)], side_convert=w_out[0])
    sgb = _spatial_gating(act_u, act_v, gmlp_ln_g, gmlp_ln_b, w_spatial[0], b_spatial[0].T)
    merged = _matmul("up_gmlp_merge", sgb, w_up_gmlp[0], _epilogue_up_gmlp, [0], D_MODEL, TM, TN,
                     tiles=[(act_merge, nw), (ya, 0)])
    out = _out_proj(merged, w_out_b, x2, final_norm_g[None, :])
    return out.reshape(b, s, d)
```

```python
import functools

import jax
import jax.numpy as jnp
from jax import lax
from jax.experimental import pallas as pl
from jax.experimental.pallas import tpu as pltpu

D_MODEL = 4096
SEQ = 4096
DEPTH = 1
N_Q_HEADS = 64
N_KV_HEADS = 8
HEAD_DIM = 64
Q_PER_KV = N_Q_HEADS // N_KV_HEADS
ATTN_WIDTH = N_Q_HEADS * HEAD_DIM
KV_WIDTH = N_KV_HEADS * HEAD_DIM
WINDOW = 128
BLOCK = 128
ROPE_THETA = 500000.0
ROPE_DIM = HEAD_DIM // 4
ROPE_HALF = ROPE_DIM // 2
GMLP_WIDTH = D_MODEL
GMLP_GROUPS = 8
GMLP_GROUP_DIM = GMLP_WIDTH // GMLP_GROUPS
GMLP_CHUNK = 128
NORM_EPS = 1e-5
LN_EPS = 1e-5

OFF_Q = 0
OFF_KV = ATTN_WIDTH
OFF_GATE_A = OFF_KV + 2 * KV_WIDTH
OFF_U = OFF_GATE_A + ATTN_WIDTH
OFF_GATE_B = OFF_U + 2 * GMLP_WIDTH
OFF_MERGE = OFF_GATE_B + GMLP_WIDTH

LANES = 128
BF16_SUBLANES = 16
VMEM_LIMIT = 56 * 1024 * 1024

TM = 1024
TN = 1024
Q_CHUNKS = 4
TM_NORM = 512
Q_TAIL = 1024
NORM_CHUNKS = 2
ATTN_BLOCKS = 4
TM_OUT = 1024
TN_OUT = 512

BF16 = jnp.bfloat16
F32 = jnp.float32


def _params(*sem):
    return pltpu.CompilerParams(dimension_semantics=sem, vmem_limit_bytes=VMEM_LIMIT)


def _rope_table_kernel(pos_ref, invf_ref, cos_ref, sin_ref):
    pad = jnp.zeros((HEAD_DIM - ROPE_DIM, LANES), F32)
    for g in range(pos_ref.shape[0]):
        ang = pos_ref[g:g + 1, :].astype(F32) * invf_ref[...]
        for fn, out_ref in ((jnp.cos, cos_ref), (jnp.sin, sin_ref)):
            val = fn(ang)
            head = jnp.concatenate([val, val, pad], axis=0)
            out_ref[g * LANES:(g + 1) * LANES, :] = jnp.concatenate([head, head], axis=0).T


def _rope_tables(pos_rows, invf_rows):
    groups = 8
    n = pos_rows.shape[0]
    return pl.pallas_call(
        _rope_table_kernel,
        grid=(n // groups,),
        in_specs=[pl.BlockSpec((groups, LANES), lambda i: (i, 0)),
                  pl.BlockSpec((ROPE_HALF, LANES), lambda i: (0, 0))],
        out_specs=[pl.BlockSpec((groups * LANES, LANES), lambda i: (i, 0)),
                   pl.BlockSpec((groups * LANES, LANES), lambda i: (i, 0))],
        out_shape=[jax.ShapeDtypeStruct((n * LANES, LANES), F32)] * 2,
        compiler_params=_params("arbitrary"),
        name="rope_tables",
    )(pos_rows, invf_rows)


def _rope_slab(a, cos, sin, lo, hi):
    up = pltpu.roll(a, LANES - ROPE_HALF, axis=1)
    dn = pltpu.roll(a, ROPE_HALF, axis=1)
    return jnp.where(lo, a * cos - up * sin, jnp.where(hi, a * cos + dn * sin, a))


def _rope_masks(rows):
    j = lax.broadcasted_iota(jnp.int32, (rows, LANES), 1) & (HEAD_DIM - 1)
    return j < ROPE_HALF, (j >= ROPE_HALF) & (j < ROPE_DIM)


def _epilogue_q(acc, o_ref, cos_ref, sin_ref):
    cos, sin = cos_ref[...], sin_ref[...]
    lo, hi = _rope_masks(acc.shape[0])
    scale = HEAD_DIM ** -0.5 * 1.4426950408889634
    for s in range(acc.shape[1] // LANES):
        sl = slice(s * LANES, (s + 1) * LANES)
        o_ref[:, sl] = (_rope_slab(acc[:, sl], cos, sin, lo, hi) * scale).astype(o_ref.dtype)


def _epilogue_qkv(acc, o_ref, cos_ref, sin_ref):
    cos, sin = cos_ref[...], sin_ref[...]
    lo, hi = _rope_masks(acc.shape[0])
    scale = HEAD_DIM ** -0.5 * 1.4426950408889634
    for s in range(acc.shape[1] // LANES):
        sl = slice(s * LANES, (s + 1) * LANES)
        a = acc[:, sl]
        if s * LANES < Q_TAIL:
            a = _rope_slab(a, cos, sin, lo, hi) * scale
        elif s * LANES < Q_TAIL + KV_WIDTH:
            a = _rope_slab(a, cos, sin, lo, hi)
        o_ref[:, sl] = a.astype(o_ref.dtype)


def _sigmoid(x):
    return 0.5 * jnp.tanh(0.5 * x) + 0.5


def _silu(x):
    half = 0.5 * x
    return half * jnp.tanh(half) + half


def _gelu(x):
    c, a = 0.7978845608028654, 0.044715
    half = 0.5 * x
    return half * jnp.tanh(x * (x * x * (c * a) + c)) + half


def _epilogue_act(acc, o_ref, *, act):
    for s in range(acc.shape[1] // LANES):
        sl = slice(s * LANES, (s + 1) * LANES)
        o_ref[:, sl] = act(acc[:, sl]).astype(o_ref.dtype)


def _epilogue_gelu_gated(acc, o_ref, gate_ref):
    for s in range(acc.shape[1] // LANES):
        sl = slice(s * LANES, (s + 1) * LANES)
        o_ref[:, sl] = (_gelu(acc[:, sl]) * gate_ref[:, sl].astype(F32)).astype(o_ref.dtype)


def _epilogue_up_attn(acc, o_ref, sa_ref):
    for s in range(acc.shape[1] // LANES):
        sl = slice(s * LANES, (s + 1) * LANES)
        o_ref[:, sl] = (sa_ref[:, sl].astype(F32) * acc[:, sl]).astype(o_ref.dtype)


def _epilogue_up_gmlp(acc, o_ref, sb_ref, ya_ref):
    for s in range(acc.shape[1] // LANES):
        sl = slice(s * LANES, (s + 1) * LANES)
        merged = ya_ref[:, sl].astype(F32) + sb_ref[:, sl].astype(F32) * acc[:, sl]
        o_ref[:, sl] = merged.astype(o_ref.dtype)


def _matmul_kernel(a_ref, w_hbm, *rest, epilogue, n_extra, w_col, nb, mb, m_chunks, with_norm, with_side):
    if with_norm:
        g_ref, rest = rest[0], rest[1:]
    extras, rest = rest[:n_extra], rest[n_extra:]
    if with_side:
        side_in_ref, rest = rest[0], rest[1:]
    o_ref, rest = rest[0], rest[1:]
    if with_norm:
        h_ref, rest = rest[0], rest[1:]
    if with_side:
        side_out_ref, rest = rest[0], rest[1:]
        side_out_ref[...] = side_in_ref[...].astype(BF16)
    wb_ref, stage_ref, sem = rest
    n, m = pl.program_id(0), pl.program_id(1)
    k, tn = wb_ref.shape[1:]
    kc = k // mb
    cur = n % 2

    def chunk_copy(block, c, slot):
        col = pl.multiple_of(w_col(block), LANES)
        return pltpu.make_async_copy(w_hbm.at[pl.ds(c * kc, kc), pl.ds(col, tn)], stage_ref.at[slot], sem.at[slot])

    @pl.when((n == 0) & (m == 0))
    def _():
        chunk_copy(0, 0, 0).start()
        for c in range(mb):
            if c + 1 < mb:
                chunk_copy(0, c + 1, (c + 1) % 2).start()
            chunk_copy(0, c, c % 2).wait()
            wb_ref[0, c * kc:(c + 1) * kc, :] = stage_ref[c % 2].astype(BF16)

    if nb > 1:
        @pl.when(n + 1 < nb)
        def _():
            chunk_copy(n + 1, m, m % 2).start()

        @pl.when((n + 1 < nb) & (m > 0))
        def _():
            chunk_copy(n + 1, m - 1, (m - 1) % 2).wait()
            row = pl.multiple_of((m - 1) * kc, kc)
            wb_ref[1 - cur, pl.ds(row, kc), :] = stage_ref[(m - 1) % 2].astype(BF16)

        @pl.when((n > 0) & (m == 0))
        def _():
            chunk_copy(n, mb - 1, (mb - 1) % 2).wait()
            wb_ref[cur, (mb - 1) * kc:, :] = stage_ref[(mb - 1) % 2].astype(BF16)

    rows = a_ref.shape[0] // m_chunks
    for c in range(m_chunks):
        rs = pl.ds(c * rows, rows)
        lhs = a_ref[rs, :]
        if with_norm:
            ms = jnp.mean(lhs * lhs, axis=-1, keepdims=True)
            lhs = (lhs * lax.rsqrt(ms + NORM_EPS) * g_ref[...]).astype(BF16)
            h_ref[rs, :] = lhs
        acc = jnp.dot(lhs, wb_ref[cur], preferred_element_type=F32)
        epilogue(acc, o_ref.at[rs, :], *[e.at[rs, :] for e in extras])


def _matmul(name, a, w, epilogue, col_ranges, width, tm, tn, lane_tiles=(), tiles=(), m_chunks=1,
            norm_gain=None, side_convert=None):
    t, k = a.shape
    per_range = width // tn
    nb, mb = per_range * len(col_ranges), t // tm
    with_norm = norm_gain is not None
    with_side = side_convert is not None
    assert not with_norm or nb == 1

    def w_col(n):
        col = col_ranges[0]
        for r in range(1, len(col_ranges)):
            col = jnp.where(n // per_range == r, col_ranges[r], col)
        return col + n % per_range * tn

    extra_specs = [pl.BlockSpec((tm, LANES), lambda n, m: (m, 0)) for _ in lane_tiles]
    extra_specs += [pl.BlockSpec((tm, tn), functools.partial(lambda n, m, off: (m, n + off), off=off))
                    for _, off in tiles]
    extras = list(lane_tiles) + [arr for arr, _ in tiles]
    body = functools.partial(_matmul_kernel, epilogue=epilogue, n_extra=len(extras), w_col=w_col, nb=nb, mb=mb,
                             m_chunks=m_chunks, with_norm=with_norm, with_side=with_side)
    in_specs = [pl.BlockSpec((tm, k), lambda n, m: (m, 0)), pl.BlockSpec(memory_space=pl.ANY)]
    out_specs = [pl.BlockSpec((tm, tn), lambda n, m: (m, n))]
    out_shape = [jax.ShapeDtypeStruct((t, nb * tn), BF16)]
    operands = [a, w]
    if with_norm:
        in_specs.append(pl.BlockSpec((1, k), lambda n, m: (0, 0)))
        operands.append(norm_gain)
        out_specs.append(pl.BlockSpec((tm, k), lambda n, m: (m, 0)))
        out_shape.append(jax.ShapeDtypeStruct((t, k), BF16))
    in_specs += extra_specs
    operands += extras
    if with_side:
        slab = (side_convert.shape[0] // (nb * mb), side_convert.shape[1])
        in_specs.append(pl.BlockSpec(slab, lambda n, m: (n * mb + m, 0)))
        operands.append(side_convert)
        out_specs.append(pl.BlockSpec(slab, lambda n, m: (n * mb + m, 0)))
        out_shape.append(jax.ShapeDtypeStruct(side_convert.shape, BF16))
    outs = pl.pallas_call(
        body,
        grid=(nb, mb),
        in_specs=in_specs,
        out_specs=out_specs,
        out_shape=out_shape,
        scratch_shapes=[pltpu.VMEM((min(nb, 2), k, tn), BF16), pltpu.VMEM((2, k // mb, tn), F32),
                        pltpu.SemaphoreType.DMA((2,))],
        compiler_params=_params("arbitrary", "arbitrary"),
        name=name,
    )(*operands)
    return outs if len(outs) > 1 else outs[0]


def _attn_body(sink_ref, q_ref, qt_ref, kvp_ref, kvc_ref, ga_ref, o_ref, mprev_ref, mcur_ref, has_prev):
    pairs = Q_PER_KV // 2
    width = pairs * LANES
    key = lax.broadcasted_iota(jnp.int32, (BLOCK, width), 0)
    qry = lax.broadcasted_iota(jnp.int32, (BLOCK, width), 1) % LANES
    from_prev = key > qry
    left = lax.broadcasted_iota(jnp.int32, (BLOCK, LANES), 1) < HEAD_DIM
    lane_pair = lax.broadcasted_iota(jnp.int32, (1, width), 1) // LANES
    n_keys = (2 if has_prev else 1) * BLOCK
    zeros_t = jnp.zeros((HEAD_DIM, n_keys), F32)
    nt = (((1,), (1,)), ((), ()))
    log2e = 1.4426950408889634

    def exact_zero(x):
        bits = lax.bitcast_convert_type(x, jnp.uint32)
        return lax.shift_right_logical(lax.shift_right_logical(bits, jnp.uint32(16)), jnp.uint32(16))

    def head_halves(slab, e):
        swap = pltpu.roll(slab, HEAD_DIM, axis=1)
        zero = jnp.zeros_like(slab)
        if e == 0:
            return jnp.where(left, slab, zero).astype(BF16), jnp.where(left, zero, swap).astype(BF16)
        return jnp.where(left, swap, zero).astype(BF16), jnp.where(left, zero, slab).astype(BF16)

    def merged_max(s_prev, s_cur, sink):
        s = jnp.where(from_prev, s_prev if has_prev else -jnp.inf, s_cur)
        return s, jnp.maximum(jnp.max(s, axis=0, keepdims=True), sink)

    def probs(s, m):
        p = jnp.exp2(s - m).astype(BF16)
        return ([p * mprev_ref[...]] if has_prev else []) + [p * mcur_ref[...]]

    sum_row = lax.broadcasted_iota(jnp.int32, (BF16_SUBLANES, 2 * n_keys), 0)
    sum_col = lax.broadcasted_iota(jnp.int32, (BF16_SUBLANES, 2 * n_keys), 1)
    sum_rows = jnp.where((sum_row == 0) == (sum_col < n_keys), 1.0, 0.0) * (sum_row < 2)

    v_t = {}

    def scores(kv_head):
        j, e = divmod(kv_head, 2)
        kcol, vcol = j * LANES, KV_WIDTH + j * LANES
        if e == 0:
            v_rows = ([kvp_ref[:, vcol:vcol + LANES]] if has_prev else []) + [kvc_ref[:, vcol:vcol + LANES]]
            v_t[j] = jnp.concatenate(v_rows, axis=0).astype(F32).T
        kc_l, kc_r = head_halves(kvc_ref[:, kcol:kcol + LANES].astype(F32), e)
        if has_prev:
            kp_l, kp_r = head_halves(kvp_ref[:, kcol:kcol + LANES].astype(F32), e)
            k_rows = jnp.concatenate([kp_l, kc_l, kp_r, kc_r], axis=0)
        else:
            k_rows = jnp.concatenate([kc_l, kc_r], axis=0)
        slabs = [slice((kv_head * pairs + pp) * LANES, (kv_head * pairs + pp + 1) * LANES) for pp in range(pairs)]
        main = q_ref.shape[1]
        q_rows = jnp.concatenate([q_ref[:, sl] if sl.start < main else qt_ref[:, sl.start - main:sl.stop - main]
                                  for sl in slabs], axis=0)
        sink_l = jnp.zeros((1, width), F32)
        sink_r = jnp.zeros((1, width), F32)
        for pp in range(pairs):
            head = 2 * (kv_head * pairs + pp)
            sink_l = jnp.where(lane_pair == pp, sink_ref[head] * log2e, sink_l)
            sink_r = jnp.where(lane_pair == pp, sink_ref[head + 1] * log2e, sink_r)
        s = lax.dot_general(k_rows, q_rows, nt, preferred_element_type=F32)
        tiles = [s[i * BLOCK:(i + 1) * BLOCK] for i in range(s.shape[0] // BLOCK)]
        if has_prev:
            s_l, m_l = merged_max(tiles[0], tiles[1], sink_l)
            s_r, m_r = merged_max(tiles[2], tiles[3], sink_r)
        else:
            s_l, m_l = merged_max(None, tiles[0], sink_l)
            s_r, m_r = merged_max(None, tiles[1], sink_r)
        return dict(vt=v_t[j][e * HEAD_DIM:(e + 1) * HEAD_DIM], slabs=slabs, s_l=s_l, s_r=s_r, m_l=m_l, m_r=m_r,
                    sink_l=sink_l, sink_r=sink_r)

    def finish(st, after):
        vt = st["vt"]
        ones = sum_rows
        if after is not None:
            zero = exact_zero(after[0]) | exact_zero(after[1])
            ones = lax.bitcast_convert_type(lax.bitcast_convert_type(ones, jnp.uint32) + zero[:, :2 * n_keys], F32)
        v_bd = jnp.concatenate([jnp.concatenate([vt, zeros_t], axis=1),
                                jnp.concatenate([zeros_t, vt], axis=1), ones], axis=0).astype(BF16)
        p2 = jnp.concatenate(probs(st["s_l"], st["m_l"]) + probs(st["s_r"], st["m_r"]), axis=0)
        o_t = jnp.dot(v_bd, p2, preferred_element_type=F32)
        d_l = o_t[2 * HEAD_DIM:2 * HEAD_DIM + 1] + jnp.exp2(st["sink_l"] - st["m_l"])
        d_r = o_t[2 * HEAD_DIM + 1:2 * HEAD_DIM + 2] + jnp.exp2(st["sink_r"] - st["m_r"])
        o_t = jnp.concatenate([o_t[:HEAD_DIM] * (1.0 / d_l), o_t[HEAD_DIM:2 * HEAD_DIM] * (1.0 / d_r)], axis=0)
        for pp, sl in enumerate(st["slabs"]):
            o = o_t[:, pp * LANES:(pp + 1) * LANES].T
            o_ref[:, sl] = (o * ga_ref[:, sl].astype(F32)).astype(o_ref.dtype)

    st = scores(0)
    for kv_head in range(N_KV_HEADS):
        nxt = scores(kv_head + 1) if kv_head + 1 < N_KV_HEADS else None
        finish(st, (nxt["m_l"], nxt["m_r"]) if nxt is not None else None)
        st = nxt


def _attn_kernel(sink_ref, q_ref, qt_ref, kvp_ref, kvc_ref, ga_ref, o_ref, mprev_ref, mcur_ref):
    assert WINDOW == BLOCK
    blk = pl.program_id(0) % (SEQ // (ATTN_BLOCKS * BLOCK))
    width = mprev_ref.shape[1]
    key = lax.broadcasted_iota(jnp.int32, (BLOCK, width), 0)
    qry = lax.broadcasted_iota(jnp.int32, (BLOCK, width), 1) % LANES
    mprev_ref[...] = (key > qry).astype(BF16)
    mcur_ref[...] = (key <= qry).astype(BF16)
    def block(b, kv_prev, has_prev):
        rows = pl.ds(b * BLOCK, BLOCK)
        _attn_body(sink_ref, q_ref.at[rows, :], qt_ref.at[rows, :], kv_prev, kvc_ref.at[rows, :], ga_ref.at[rows, :],
                   o_ref.at[rows, :], mprev_ref, mcur_ref, has_prev=has_prev)

    pl.when(blk > 0)(lambda: block(0, kvp_ref, True))
    pl.when(blk == 0)(lambda: block(0, kvp_ref, False))
    for b in range(1, ATTN_BLOCKS):
        block(b, kvc_ref.at[pl.ds((b - 1) * BLOCK, BLOCK), :], True)


def _attention(q, qkv, act_a, sink):
    t = q.shape[0]
    rows = ATTN_BLOCKS * BLOCK
    assert Q_TAIL == 2 * KV_WIDTH
    return pl.pallas_call(
        _attn_kernel,
        grid_spec=pltpu.PrefetchScalarGridSpec(
            num_scalar_prefetch=1,
            grid=(t // rows,),
            in_specs=[pl.BlockSpec((rows, ATTN_WIDTH - Q_TAIL), lambda i, s: (i, 0)),
                      pl.BlockSpec((rows, Q_TAIL), lambda i, s: (i, 0)),
                      pl.BlockSpec((BLOCK, 2 * KV_WIDTH), lambda i, s: (jnp.maximum(i * ATTN_BLOCKS - 1, 0), 1)),
                      pl.BlockSpec((rows, 2 * KV_WIDTH), lambda i, s: (i, 1)),
                      pl.BlockSpec((rows, ATTN_WIDTH), lambda i, s: (i, 0))],
            out_specs=pl.BlockSpec((rows, ATTN_WIDTH), lambda i, s: (i, 0)),
            scratch_shapes=[pltpu.VMEM((BLOCK, (Q_PER_KV // 2) * LANES), BF16)] * 2,
        ),
        out_shape=jax.ShapeDtypeStruct((t, ATTN_WIDTH), BF16),
        compiler_params=_params("arbitrary"),
        name="swa_sink_attention",
    )(sink, q, qkv, qkv, qkv, act_a)


def _gating_kernel(u_ref, v_ref, lng_ref, lnb_ref, ws_ref, bt_ref, o_ref):
    v = v_ref[...].astype(F32)
    mu = jnp.mean(v, axis=-1, keepdims=True)
    vc = v - mu
    var = jnp.mean(vc * vc, axis=-1, keepdims=True)
    vn = (vc * lax.rsqrt(var + LN_EPS) * lng_ref[...] + lnb_ref[...]).astype(BF16)
    ti = lax.broadcasted_iota(jnp.int32, (GMLP_CHUNK, GMLP_CHUNK), 0)
    si = lax.broadcasted_iota(jnp.int32, (GMLP_CHUNK, GMLP_CHUNK), 1)
    causal = si <= ti
    bt = bt_ref[...]
    for g in range(GMLP_GROUPS):
        w = jnp.where(causal, ws_ref[g], 0.0).astype(BF16)
        bias = bt[:, g:g + 1]
        cols = slice(g * GMLP_GROUP_DIM, (g + 1) * GMLP_GROUP_DIM)
        for c in range(v.shape[0] // GMLP_CHUNK):
            rows = slice(c * GMLP_CHUNK, (c + 1) * GMLP_CHUNK)
            mixed = jnp.dot(w, vn[rows, cols], preferred_element_type=F32) + bias
            o_ref[rows, cols] = (u_ref[rows, cols].astype(F32) * mixed).astype(o_ref.dtype)


def _spatial_gating(act_u, act_v, ln_g, ln_b, w_s, b_t):
    t = act_u.shape[0]
    r = 4 * GMLP_CHUNK
    w = GMLP_WIDTH
    return pl.pallas_call(
        _gating_kernel,
        grid=(t // r,),
        in_specs=[pl.BlockSpec((r, w), lambda i: (i, 0)),
                  pl.BlockSpec((r, w), lambda i: (i, 0)),
                  pl.BlockSpec((1, w), lambda i: (0, 0)),
                  pl.BlockSpec((1, w), lambda i: (0, 0)),
                  pl.BlockSpec((GMLP_GROUPS, GMLP_CHUNK, GMLP_CHUNK), lambda i: (0, 0, 0)),
                  pl.BlockSpec((GMLP_CHUNK, GMLP_GROUPS), lambda i: (0, 0))],
        out_specs=pl.BlockSpec((r, w), lambda i: (i, 0)),
        out_shape=jax.ShapeDtypeStruct((t, w), BF16),
        compiler_params=_params("arbitrary"),
        name="spatial_gating",
    )(act_u, act_v, ln_g, ln_b, w_s, b_t)


def _out_kernel(a_ref, w_ref, x_ref, g_ref, o_ref, y_ref, ss_ref, *, n_row_blocks):
    i, j = pl.program_id(0), pl.program_id(1)
    tn = x_ref.shape[1]
    col = pl.multiple_of(j * tn, tn)
    cur = i % 2

    @pl.when(i > 0)
    def _():
        ms_prev = jnp.sum(ss_ref[1 - cur], axis=-1, keepdims=True) * (1.0 / D_MODEL)
        o_ref[...] = y_ref[:, pl.ds(col, tn)] * lax.rsqrt(ms_prev + NORM_EPS) * g_ref[:, pl.ds(col, tn)]

    @pl.when(i < n_row_blocks)
    def _():
        y = x_ref[...] + jnp.dot(a_ref[...], w_ref[...], preferred_element_type=F32)
        sq = y * y
        part = sq[:, :LANES]
        for s in range(1, tn // LANES):
            part = part + sq[:, s * LANES:(s + 1) * LANES]
        ss_ref[cur] = jnp.where(j == 0, part, ss_ref[cur] + part)
        y_ref[:, pl.ds(col, tn)] = y


def _out_proj(a, w, x2, g_row):
    t, k = a.shape
    n = w.shape[1]
    last = t // TM_OUT - 1
    return pl.pallas_call(
        functools.partial(_out_kernel, n_row_blocks=last + 1),
        grid=(t // TM_OUT + 1, n // TN_OUT),
        in_specs=[pl.BlockSpec((TM_OUT, k), lambda i, j: (jnp.minimum(i, last), 0)),
                  pl.BlockSpec((k, TN_OUT), lambda i, j: (0, j)),
                  pl.BlockSpec((TM_OUT, TN_OUT), lambda i, j: (jnp.minimum(i, last), j)),
                  pl.BlockSpec((1, n), lambda i, j: (0, 0))],
        out_specs=pl.BlockSpec((TM_OUT, TN_OUT), lambda i, j: (jnp.maximum(i - 1, 0), jnp.where(i == 0, 0, j))),
        out_shape=jax.ShapeDtypeStruct((t, n), F32),
        scratch_shapes=[pltpu.VMEM((TM_OUT, n), F32), pltpu.VMEM((2, TM_OUT, LANES), F32)],
        compiler_params=_params("arbitrary", "arbitrary"),
        name="out_proj_residual_norm",
    )(a, w, x2, g_row)


def _rope_inv_freq_rows():
    inv_freq = ROPE_THETA ** (-jnp.arange(ROPE_HALF, dtype=F32) * 2.0 / ROPE_DIM)
    return jnp.broadcast_to(inv_freq[:, None], (ROPE_HALF, LANES))


def kernel(x, positions, norm_g, w_in, attn_sink, gmlp_ln_g, gmlp_ln_b, w_spatial, b_spatial, w_up_attn,
           w_up_gmlp, w_out, final_norm_g):
    b, s, d = x.shape
    t = b * s
    assert s == SEQ and d == D_MODEL and norm_g.shape[0] == DEPTH == 1
    x2 = x.reshape(t, d)
    cos_t, sin_t = _rope_tables(positions.reshape(t // LANES, LANES), _rope_inv_freq_rows())
    nw = D_MODEL // TN
    w_i = w_in[0]
    qkv_w = Q_TAIL + 2 * KV_WIDTH
    qkv, h = _matmul("norm_in_proj_qkv", x2, w_i, _epilogue_qkv, [OFF_KV - Q_TAIL], qkv_w, TM_NORM, qkv_w,
                     lane_tiles=(cos_t, sin_t), m_chunks=NORM_CHUNKS, norm_gain=norm_g)
    q = _matmul("in_proj_q", h, w_i, _epilogue_q, [OFF_Q], ATTN_WIDTH - Q_TAIL, TM, TN, lane_tiles=(cos_t, sin_t),
                m_chunks=Q_CHUNKS)
    act_gate = _matmul("in_proj_silu", h, w_i, functools.partial(_epilogue_act, act=_silu),
                       [OFF_GATE_A, OFF_GATE_B], ATTN_WIDTH, TM, TN)
    act_u = _matmul("in_proj_gelu_u", h, w_i, _epilogue_gelu_gated, [OFF_U], GMLP_WIDTH, TM, TN,
                    tiles=[(act_gate, nw)])
    act_v = _matmul("in_proj_gelu_v", h, w_i, functools.partial(_epilogue_act, act=_gelu),
                    [OFF_U + GMLP_WIDTH], GMLP_WIDTH, TM, TN)
    act_merge = _matmul("in_proj_sigmoid", h, w_i, functools.partial(_epilogue_act, act=_sigmoid),
                        [OFF_MERGE], 2 * D_MODEL, TM, TN)
    attn_g = _attention(q, qkv, act_gate, attn_sink[0])
    ya, w_out_b = _matmul("up_attn", attn_g, w_up_attn[0], _epilogue_up_attn, [0], D_MODEL, TM, TN,
                          tiles=[(act_merge, 0)], side_convert=w_out[0])
    sgb = _spatial_gating(act_u, act_v, gmlp_ln_g, gmlp_ln_b, w_spatial[0], b_spatial[0].T)
    merged = _matmul("up_gmlp_merge", sgb, w_up_gmlp[0], _epilogue_up_gmlp, [0], D_MODEL, TM, TN,
                     tiles=[(act_merge, nw), (ya, 0)])
    out = _out_proj(merged, w_out_b, x2, final_norm_g[None, :])
    return out.reshape(b, s, d)
```

```python
import functools

import jax
import jax.numpy as jnp
from jax import lax
from jax.experimental import pallas as pl
from jax.experimental.pallas import tpu as pltpu

D_MODEL = 4096
SEQ = 4096
DEPTH = 1
N_Q_HEADS = 64
N_KV_HEADS = 8
HEAD_DIM = 64
Q_PER_KV = N_Q_HEADS // N_KV_HEADS
ATTN_WIDTH = N_Q_HEADS * HEAD_DIM
KV_WIDTH = N_KV_HEADS * HEAD_DIM
WINDOW = 128
BLOCK = 128
ROPE_THETA = 500000.0
ROPE_DIM = HEAD_DIM // 4
ROPE_HALF = ROPE_DIM // 2
GMLP_WIDTH = D_MODEL
GMLP_GROUPS = 8
GMLP_GROUP_DIM = GMLP_WIDTH // GMLP_GROUPS
GMLP_CHUNK = 128
NORM_EPS = 1e-5
LN_EPS = 1e-5

OFF_Q = 0
OFF_KV = ATTN_WIDTH
OFF_GATE_A = OFF_KV + 2 * KV_WIDTH
OFF_U = OFF_GATE_A + ATTN_WIDTH
OFF_GATE_B = OFF_U + 2 * GMLP_WIDTH
OFF_MERGE = OFF_GATE_B + GMLP_WIDTH

LANES = 128
BF16_SUBLANES = 16
VMEM_LIMIT = 56 * 1024 * 1024

TM = 1024
TN = 1024
Q_CHUNKS = 4
TM_NORM = 512
Q_TAIL = 1024
NORM_CHUNKS = 2
ATTN_BLOCKS = 2
TM_OUT = 1024
TN_OUT = 512

BF16 = jnp.bfloat16
F32 = jnp.float32


def _params(*sem):
    return pltpu.CompilerParams(dimension_semantics=sem, vmem_limit_bytes=VMEM_LIMIT)


def _rope_table_kernel(pos_ref, invf_ref, cos_ref, sin_ref):
    pad = jnp.zeros((HEAD_DIM - ROPE_DIM, LANES), F32)
    for g in range(pos_ref.shape[0]):
        ang = pos_ref[g:g + 1, :].astype(F32) * invf_ref[...]
        for fn, out_ref in ((jnp.cos, cos_ref), (jnp.sin, sin_ref)):
            val = fn(ang)
            head = jnp.concatenate([val, val, pad], axis=0)
            out_ref[g * LANES:(g + 1) * LANES, :] = jnp.concatenate([head, head], axis=0).T


def _rope_tables(pos_rows, invf_rows):
    groups = 8
    n = pos_rows.shape[0]
    return pl.pallas_call(
        _rope_table_kernel,
        grid=(n // groups,),
        in_specs=[pl.BlockSpec((groups, LANES), lambda i: (i, 0)),
                  pl.BlockSpec((ROPE_HALF, LANES), lambda i: (0, 0))],
        out_specs=[pl.BlockSpec((groups * LANES, LANES), lambda i: (i, 0)),
                   pl.BlockSpec((groups * LANES, LANES), lambda i: (i, 0))],
        out_shape=[jax.ShapeDtypeStruct((n * LANES, LANES), F32)] * 2,
        compiler_params=_params("arbitrary"),
        name="rope_tables",
    )(pos_rows, invf_rows)


def _rope_slab(a, cos, sin, lo, hi):
    up = pltpu.roll(a, LANES - ROPE_HALF, axis=1)
    dn = pltpu.roll(a, ROPE_HALF, axis=1)
    return jnp.where(lo, a * cos - up * sin, jnp.where(hi, a * cos + dn * sin, a))


def _rope_masks(rows):
    j = lax.broadcasted_iota(jnp.int32, (rows, LANES), 1) & (HEAD_DIM - 1)
    return j < ROPE_HALF, (j >= ROPE_HALF) & (j < ROPE_DIM)


def _epilogue_q(acc, o_ref, cos_ref, sin_ref):
    cos, sin = cos_ref[...], sin_ref[...]
    lo, hi = _rope_masks(acc.shape[0])
    scale = HEAD_DIM ** -0.5 * 1.4426950408889634
    for s in range(acc.shape[1] // LANES):
        sl = slice(s * LANES, (s + 1) * LANES)
        o_ref[:, sl] = (_rope_slab(acc[:, sl], cos, sin, lo, hi) * scale).astype(o_ref.dtype)


def _epilogue_qkv(acc, o_ref, cos_ref, sin_ref):
    cos, sin = cos_ref[...], sin_ref[...]
    lo, hi = _rope_masks(acc.shape[0])
    scale = HEAD_DIM ** -0.5 * 1.4426950408889634
    for s in range(acc.shape[1] // LANES):
        sl = slice(s * LANES, (s + 1) * LANES)
        a = acc[:, sl]
        if s * LANES < Q_TAIL:
            a = _rope_slab(a, cos, sin, lo, hi) * scale
        elif s * LANES < Q_TAIL + KV_WIDTH:
            a = _rope_slab(a, cos, sin, lo, hi)
        o_ref[:, sl] = a.astype(o_ref.dtype)


def _sigmoid(x):
    return 0.5 * jnp.tanh(0.5 * x) + 0.5


def _silu(x):
    half = 0.5 * x
    return half * jnp.tanh(half) + half


def _gelu(x):
    c, a = 0.7978845608028654, 0.044715
    half = 0.5 * x
    return half * jnp.tanh(x * (x * x * (c * a) + c)) + half


def _epilogue_act(acc, o_ref, *, act):
    for s in range(acc.shape[1] // LANES):
        sl = slice(s * LANES, (s + 1) * LANES)
        o_ref[:, sl] = act(acc[:, sl]).astype(o_ref.dtype)


def _epilogue_gelu_gated(acc, o_ref, gate_ref):
    for s in range(acc.shape[1] // LANES):
        sl = slice(s * LANES, (s + 1) * LANES)
        o_ref[:, sl] = (_gelu(acc[:, sl]) * gate_ref[:, sl].astype(F32)).astype(o_ref.dtype)


def _epilogue_up_attn(acc, o_ref, sa_ref):
    for s in range(acc.shape[1] // LANES):
        sl = slice(s * LANES, (s + 1) * LANES)
        o_ref[:, sl] = (sa_ref[:, sl].astype(F32) * acc[:, sl]).astype(o_ref.dtype)


def _epilogue_up_gmlp(acc, o_ref, sb_ref, ya_ref):
    for s in range(acc.shape[1] // LANES):
        sl = slice(s * LANES, (s + 1) * LANES)
        merged = ya_ref[:, sl].astype(F32) + sb_ref[:, sl].astype(F32) * acc[:, sl]
        o_ref[:, sl] = merged.astype(o_ref.dtype)


def _matmul_kernel(a_ref, w_hbm, *rest, epilogue, n_extra, w_col, nb, mb, m_chunks, with_norm, with_side):
    if with_norm:
        g_ref, rest = rest[0], rest[1:]
    extras, rest = rest[:n_extra], rest[n_extra:]
    if with_side:
        side_in_ref, rest = rest[0], rest[1:]
    o_ref, rest = rest[0], rest[1:]
    if with_norm:
        h_ref, rest = rest[0], rest[1:]
    if with_side:
        side_out_ref, rest = rest[0], rest[1:]
        side_out_ref[...] = side_in_ref[...].astype(BF16)
    wb_ref, stage_ref, sem = rest
    n, m = pl.program_id(0), pl.program_id(1)
    k, tn = wb_ref.shape[1:]
    kc = k // mb
    cur = n % 2

    def chunk_copy(block, c, slot):
        col = pl.multiple_of(w_col(block), LANES)
        return pltpu.make_async_copy(w_hbm.at[pl.ds(c * kc, kc), pl.ds(col, tn)], stage_ref.at[slot], sem.at[slot])

    @pl.when((n == 0) & (m == 0))
    def _():
        chunk_copy(0, 0, 0).start()
        for c in range(mb):
            if c + 1 < mb:
                chunk_copy(0, c + 1, (c + 1) % 2).start()
            chunk_copy(0, c, c % 2).wait()
            wb_ref[0, c * kc:(c + 1) * kc, :] = stage_ref[c % 2].astype(BF16)

    if nb > 1:
        @pl.when(n + 1 < nb)
        def _():
            chunk_copy(n + 1, m, m % 2).start(priority=1)

        @pl.when((n + 1 < nb) & (m > 0))
        def _():
            chunk_copy(n + 1, m - 1, (m - 1) % 2).wait()
            row = pl.multiple_of((m - 1) * kc, kc)
            wb_ref[1 - cur, pl.ds(row, kc), :] = stage_ref[(m - 1) % 2].astype(BF16)

        @pl.when((n > 0) & (m == 0))
        def _():
            chunk_copy(n, mb - 1, (mb - 1) % 2).wait()
            wb_ref[cur, (mb - 1) * kc:, :] = stage_ref[(mb - 1) % 2].astype(BF16)

    rows = a_ref.shape[0] // m_chunks
    for c in range(m_chunks):
        rs = pl.ds(c * rows, rows)
        lhs = a_ref[rs, :]
        if with_norm:
            ms = jnp.mean(lhs * lhs, axis=-1, keepdims=True)
            lhs = (lhs * lax.rsqrt(ms + NORM_EPS) * g_ref[...]).astype(BF16)
            h_ref[rs, :] = lhs
        acc = jnp.dot(lhs, wb_ref[cur], preferred_element_type=F32)
        epilogue(acc, o_ref.at[rs, :], *[e.at[rs, :] for e in extras])


def _matmul(name, a, w, epilogue, col_ranges, width, tm, tn, lane_tiles=(), tiles=(), m_chunks=1,
            norm_gain=None, side_convert=None):
    t, k = a.shape
    per_range = width // tn
    nb, mb = per_range * len(col_ranges), t // tm
    with_norm = norm_gain is not None
    with_side = side_convert is not None
    assert not with_norm or nb == 1

    def w_col(n):
        col = col_ranges[0]
        for r in range(1, len(col_ranges)):
            col = jnp.where(n // per_range == r, col_ranges[r], col)
        return col + n % per_range * tn

    extra_specs = [pl.BlockSpec((tm, LANES), lambda n, m: (m, 0)) for _ in lane_tiles]
    extra_specs += [pl.BlockSpec((tm, tn), functools.partial(lambda n, m, off: (m, n + off), off=off))
                    for _, off in tiles]
    extras = list(lane_tiles) + [arr for arr, _ in tiles]
    body = functools.partial(_matmul_kernel, epilogue=epilogue, n_extra=len(extras), w_col=w_col, nb=nb, mb=mb,
                             m_chunks=m_chunks, with_norm=with_norm, with_side=with_side)
    in_specs = [pl.BlockSpec((tm, k), lambda n, m: (m, 0)), pl.BlockSpec(memory_space=pl.ANY)]
    out_specs = [pl.BlockSpec((tm, tn), lambda n, m: (m, n))]
    out_shape = [jax.ShapeDtypeStruct((t, nb * tn), BF16)]
    operands = [a, w]
    if with_norm:
        in_specs.append(pl.BlockSpec((1, k), lambda n, m: (0, 0)))
        operands.append(norm_gain)
        out_specs.append(pl.BlockSpec((tm, k), lambda n, m: (m, 0)))
        out_shape.append(jax.ShapeDtypeStruct((t, k), BF16))
    in_specs += extra_specs
    operands += extras
    if with_side:
        slab = (side_convert.shape[0] // (nb * mb), side_convert.shape[1])
        in_specs.append(pl.BlockSpec(slab, lambda n, m: (n * mb + m, 0)))
        operands.append(side_convert)
        out_specs.append(pl.BlockSpec(slab, lambda n, m: (n * mb + m, 0)))
        out_shape.append(jax.ShapeDtypeStruct(side_convert.shape, BF16))
    outs = pl.pallas_call(
        body,
        grid=(nb, mb),
        in_specs=in_specs,
        out_specs=out_specs,
        out_shape=out_shape,
        scratch_shapes=[pltpu.VMEM((min(nb, 2), k, tn), BF16), pltpu.VMEM((2, k // mb, tn), F32),
                        pltpu.SemaphoreType.DMA((2,))],
        compiler_params=_params("arbitrary", "arbitrary"),
        name=name,
    )(*operands)
    return outs if len(outs) > 1 else outs[0]


def _attn_body(sink_ref, q_ref, qt_ref, kvp_ref, kvc_ref, ga_ref, o_ref, mprev_ref, mcur_ref, has_prev):
    pairs = Q_PER_KV // 2
    width = pairs * LANES
    key = lax.broadcasted_iota(jnp.int32, (BLOCK, width), 0)
    qry = lax.broadcasted_iota(jnp.int32, (BLOCK, width), 1) % LANES
    from_prev = key > qry
    left = lax.broadcasted_iota(jnp.int32, (BLOCK, LANES), 1) < HEAD_DIM
    lane_pair = lax.broadcasted_iota(jnp.int32, (1, width), 1) // LANES
    n_keys = (2 if has_prev else 1) * BLOCK
    zeros_t = jnp.zeros((HEAD_DIM, n_keys), F32)
    nt = (((1,), (1,)), ((), ()))
    log2e = 1.4426950408889634

    def exact_zero(x):
        bits = lax.bitcast_convert_type(x, jnp.uint32)
        return lax.shift_right_logical(lax.shift_right_logical(bits, jnp.uint32(16)), jnp.uint32(16))

    def head_halves(slab, e):
        swap = pltpu.roll(slab, HEAD_DIM, axis=1)
        zero = jnp.zeros_like(slab)
        if e == 0:
            return jnp.where(left, slab, zero).astype(BF16), jnp.where(left, zero, swap).astype(BF16)
        return jnp.where(left, swap, zero).astype(BF16), jnp.where(left, zero, slab).astype(BF16)

    def merged_max(s_prev, s_cur, sink):
        s = jnp.where(from_prev, s_prev if has_prev else -jnp.inf, s_cur)
        return s, jnp.maximum(jnp.max(s, axis=0, keepdims=True), sink)

    def probs(s, m):
        p = jnp.exp2(s - m).astype(BF16)
        return ([p * mprev_ref[...]] if has_prev else []) + [p * mcur_ref[...]]

    sum_row = lax.broadcasted_iota(jnp.int32, (BF16_SUBLANES, 2 * n_keys), 0)
    sum_col = lax.broadcasted_iota(jnp.int32, (BF16_SUBLANES, 2 * n_keys), 1)
    sum_rows = jnp.where((sum_row == 0) == (sum_col < n_keys), 1.0, 0.0) * (sum_row < 2)

    v_t = {}

    def scores(kv_head):
        j, e = divmod(kv_head, 2)
        kcol, vcol = j * LANES, KV_WIDTH + j * LANES
        if e == 0:
            v_rows = ([kvp_ref[:, vcol:vcol + LANES]] if has_prev else []) + [kvc_ref[:, vcol:vcol + LANES]]
            v_t[j] = jnp.concatenate(v_rows, axis=0).astype(F32).T
        kc_l, kc_r = head_halves(kvc_ref[:, kcol:kcol + LANES].astype(F32), e)
        if has_prev:
            kp_l, kp_r = head_halves(kvp_ref[:, kcol:kcol + LANES].astype(F32), e)
            k_rows = jnp.concatenate([kp_l, kc_l, kp_r, kc_r], axis=0)
        else:
            k_rows = jnp.concatenate([kc_l, kc_r], axis=0)
        slabs = [slice((kv_head * pairs + pp) * LANES, (kv_head * pairs + pp + 1) * LANES) for pp in range(pairs)]
        main = q_ref.shape[1]
        q_rows = jnp.concatenate([q_ref[:, sl] if sl.start < main else qt_ref[:, sl.start - main:sl.stop - main]
                                  for sl in slabs], axis=0)
        sink_l = jnp.zeros((1, width), F32)
        sink_r = jnp.zeros((1, width), F32)
        for pp in range(pairs):
            head = 2 * (kv_head * pairs + pp)
            sink_l = jnp.where(lane_pair == pp, sink_ref[head] * log2e, sink_l)
            sink_r = jnp.where(lane_pair == pp, sink_ref[head + 1] * log2e, sink_r)
        s = lax.dot_general(k_rows, q_rows, nt, preferred_element_type=F32)
        tiles = [s[i * BLOCK:(i + 1) * BLOCK] for i in range(s.shape[0] // BLOCK)]
        if has_prev:
            s_l, m_l = merged_max(tiles[0], tiles[1], sink_l)
            s_r, m_r = merged_max(tiles[2], tiles[3], sink_r)
        else:
            s_l, m_l = merged_max(None, tiles[0], sink_l)
            s_r, m_r = merged_max(None, tiles[1], sink_r)
        return dict(vt=v_t[j][e * HEAD_DIM:(e + 1) * HEAD_DIM], slabs=slabs, s_l=s_l, s_r=s_r, m_l=m_l, m_r=m_r,
                    sink_l=sink_l, sink_r=sink_r)

    def finish(st, after):
        vt = st["vt"]
        ones = sum_rows
        if after is not None:
            zero = exact_zero(after[0]) | exact_zero(after[1])
            ones = lax.bitcast_convert_type(lax.bitcast_convert_type(ones, jnp.uint32) + zero[:, :2 * n_keys], F32)
        v_bd = jnp.concatenate([jnp.concatenate([vt, zeros_t], axis=1),
                                jnp.concatenate([zeros_t, vt], axis=1), ones], axis=0).astype(BF16)
        p2 = jnp.concatenate(probs(st["s_l"], st["m_l"]) + probs(st["s_r"], st["m_r"]), axis=0)
        o_t = jnp.dot(v_bd, p2, preferred_element_type=F32)
        d_l = o_t[2 * HEAD_DIM:2 * HEAD_DIM + 1] + jnp.exp2(st["sink_l"] - st["m_l"])
        d_r = o_t[2 * HEAD_DIM + 1:2 * HEAD_DIM + 2] + jnp.exp2(st["sink_r"] - st["m_r"])
        o_t = jnp.concatenate([o_t[:HEAD_DIM] * (1.0 / d_l), o_t[HEAD_DIM:2 * HEAD_DIM] * (1.0 / d_r)], axis=0)
        for pp, sl in enumerate(st["slabs"]):
            o = o_t[:, pp * LANES:(pp + 1) * LANES].T
            o_ref[:, sl] = (o * ga_ref[:, sl].astype(F32)).astype(o_ref.dtype)

    st = scores(0)
    for kv_head in range(N_KV_HEADS):
        nxt = scores(kv_head + 1) if kv_head + 1 < N_KV_HEADS else None
        finish(st, (nxt["m_l"], nxt["m_r"]) if nxt is not None else None)
        st = nxt


def _attn_kernel(sink_ref, q_ref, qt_ref, kvp_ref, kvc_ref, ga_ref, o_ref, mprev_ref, mcur_ref):
    assert WINDOW == BLOCK
    blk = pl.program_id(0) % (SEQ // (ATTN_BLOCKS * BLOCK))
    width = mprev_ref.shape[1]
    key = lax.broadcasted_iota(jnp.int32, (BLOCK, width), 0)
    qry = lax.broadcasted_iota(jnp.int32, (BLOCK, width), 1) % LANES
    mprev_ref[...] = (key > qry).astype(BF16)
    mcur_ref[...] = (key <= qry).astype(BF16)
    def block(b, kv_prev, has_prev):
        rows = pl.ds(b * BLOCK, BLOCK)
        _attn_body(sink_ref, q_ref.at[rows, :], qt_ref.at[rows, :], kv_prev, kvc_ref.at[rows, :], ga_ref.at[rows, :],
                   o_ref.at[rows, :], mprev_ref, mcur_ref, has_prev=has_prev)

    pl.when(blk > 0)(lambda: block(0, kvp_ref, True))
    pl.when(blk == 0)(lambda: block(0, kvp_ref, False))
    for b in range(1, ATTN_BLOCKS):
        block(b, kvc_ref.at[pl.ds((b - 1) * BLOCK, BLOCK), :], True)


def _attention(q, qkv, act_a, sink):
    t = q.shape[0]
    rows = ATTN_BLOCKS * BLOCK
    assert Q_TAIL == 2 * KV_WIDTH
    return pl.pallas_call(
        _attn_kernel,
        grid_spec=pltpu.PrefetchScalarGridSpec(
            num_scalar_prefetch=1,
            grid=(t // rows,),
            in_specs=[pl.BlockSpec((rows, ATTN_WIDTH - Q_TAIL), lambda i, s: (i, 0)),
                      pl.BlockSpec((rows, Q_TAIL), lambda i, s: (i, 0)),
                      pl.BlockSpec((BLOCK, 2 * KV_WIDTH), lambda i, s: (jnp.maximum(i * ATTN_BLOCKS - 1, 0), 1)),
                      pl.BlockSpec((rows, 2 * KV_WIDTH), lambda i, s: (i, 1)),
                      pl.BlockSpec((rows, ATTN_WIDTH), lambda i, s: (i, 0))],
            out_specs=pl.BlockSpec((rows, ATTN_WIDTH), lambda i, s: (i, 0)),
            scratch_shapes=[pltpu.VMEM((BLOCK, (Q_PER_KV // 2) * LANES), BF16)] * 2,
        ),
        out_shape=jax.ShapeDtypeStruct((t, ATTN_WIDTH), BF16),
        compiler_params=_params("arbitrary"),
        name="swa_sink_attention",
    )(sink, q, qkv, qkv, qkv, act_a)


def _gating_kernel(u_ref, v_ref, lng_ref, lnb_ref, ws_ref, bt_ref, o_ref):
    v = v_ref[...].astype(F32)
    mu = jnp.mean(v, axis=-1, keepdims=True)
    vc = v - mu
    var = jnp.mean(vc * vc, axis=-1, keepdims=True)
    vn = (vc * lax.rsqrt(var + LN_EPS) * lng_ref[...] + lnb_ref[...]).astype(BF16)
    ti = lax.broadcasted_iota(jnp.int32, (GMLP_CHUNK, GMLP_CHUNK), 0)
    si = lax.broadcasted_iota(jnp.int32, (GMLP_CHUNK, GMLP_CHUNK), 1)
    causal = si <= ti
    bt = bt_ref[...]
    for g in range(GMLP_GROUPS):
        w = jnp.where(causal, ws_ref[g], 0.0).astype(BF16)
        bias = bt[:, g:g + 1]
        cols = slice(g * GMLP_GROUP_DIM, (g + 1) * GMLP_GROUP_DIM)
        for c in range(v.shape[0] // GMLP_CHUNK):
            rows = slice(c * GMLP_CHUNK, (c + 1) * GMLP_CHUNK)
            mixed = jnp.dot(w, vn[rows, cols], preferred_element_type=F32) + bias
            o_ref[rows, cols] = (u_ref[rows, cols].astype(F32) * mixed).astype(o_ref.dtype)


def _spatial_gating(act_u, act_v, ln_g, ln_b, w_s, b_t):
    t = act_u.shape[0]
    r = 4 * GMLP_CHUNK
    w = GMLP_WIDTH
    return pl.pallas_call(
        _gating_kernel,
        grid=(t // r,),
        in_specs=[pl.BlockSpec((r, w), lambda i: (i, 0)),
                  pl.BlockSpec((r, w), lambda i: (i, 0)),
                  pl.BlockSpec((1, w), lambda i: (0, 0)),
                  pl.BlockSpec((1, w), lambda i: (0, 0)),
                  pl.BlockSpec((GMLP_GROUPS, GMLP_CHUNK, GMLP_CHUNK), lambda i: (0, 0, 0)),
                  pl.BlockSpec((GMLP_CHUNK, GMLP_GROUPS), lambda i: (0, 0))],
        out_specs=pl.BlockSpec((r, w), lambda i: (i, 0)),
        out_shape=jax.ShapeDtypeStruct((t, w), BF16),
        compiler_params=_params("arbitrary"),
        name="spatial_gating",
    )(act_u, act_v, ln_g, ln_b, w_s, b_t)


def _out_kernel(a_ref, w_ref, x_ref, g_ref, o_ref, y_ref, ss_ref, *, n_row_blocks):
    i, j = pl.program_id(0), pl.program_id(1)
    tn = x_ref.shape[1]
    col = pl.multiple_of(j * tn, tn)
    cur = i % 2

    @pl.when(i > 0)
    def _():
        ms_prev = jnp.sum(ss_ref[1 - cur], axis=-1, keepdims=True) * (1.0 / D_MODEL)
        o_ref[...] = y_ref[:, pl.ds(col, tn)] * lax.rsqrt(ms_prev + NORM_EPS) * g_ref[:, pl.ds(col, tn)]

    @pl.when(i < n_row_blocks)
    def _():
        y = x_ref[...] + jnp.dot(a_ref[...], w_ref[...], preferred_element_type=F32)
        sq = y * y
        part = sq[:, :LANES]
        for s in range(1, tn // LANES):
            part = part + sq[:, s * LANES:(s + 1) * LANES]
        ss_ref[cur] = jnp.where(j == 0, part, ss_ref[cur] + part)
        y_ref[:, pl.ds(col, tn)] = y


def _out_proj(a, w, x2, g_row):
    t, k = a.shape
    n = w.shape[1]
    last = t // TM_OUT - 1
    return pl.pallas_call(
        functools.partial(_out_kernel, n_row_blocks=last + 1),
        grid=(t // TM_OUT + 1, n // TN_OUT),
        in_specs=[pl.BlockSpec((TM_OUT, k), lambda i, j: (jnp.minimum(i, last), 0)),
                  pl.BlockSpec((k, TN_OUT), lambda i, j: (0, j)),
                  pl.BlockSpec((TM_OUT, TN_OUT), lambda i, j: (jnp.minimum(i, last), j)),
                  pl.BlockSpec((1, n), lambda i, j: (0, 0))],
        out_specs=pl.BlockSpec((TM_OUT, TN_OUT), lambda i, j: (jnp.maximum(i - 1, 0), jnp.where(i == 0, 0, j))),
        out_shape=jax.ShapeDtypeStruct((t, n), F32),
        scratch_shapes=[pltpu.VMEM((TM_OUT, n), F32), pltpu.VMEM((2, TM_OUT, LANES), F32)],
        compiler_params=_params("arbitrary", "arbitrary"),
        name="out_proj_residual_norm",
    )(a, w, x2, g_row)


def _rope_inv_freq_rows():
    inv_freq = ROPE_THETA ** (-jnp.arange(ROPE_HALF, dtype=F32) * 2.0 / ROPE_DIM)
    return jnp.broadcast_to(inv_freq[:, None], (ROPE_HALF, LANES))


def kernel(x, positions, norm_g, w_in, attn_sink, gmlp_ln_g, gmlp_ln_b, w_spatial, b_spatial, w_up_attn,
           w_up_gmlp, w_out, final_norm_g):
    b, s, d = x.shape
    t = b * s
    assert s == SEQ and d == D_MODEL and norm_g.shape[0] == DEPTH == 1
    x2 = x.reshape(t, d)
    cos_t, sin_t = _rope_tables(positions.reshape(t // LANES, LANES), _rope_inv_freq_rows())
    nw = D_MODEL // TN
    w_i = w_in[0]
    qkv_w = Q_TAIL + 2 * KV_WIDTH
    qkv, h = _matmul("norm_in_proj_qkv", x2, w_i, _epilogue_qkv, [OFF_KV - Q_TAIL], qkv_w, TM_NORM, qkv_w,
                     lane_tiles=(cos_t, sin_t), m_chunks=NORM_CHUNKS, norm_gain=norm_g)
    q = _matmul("in_proj_q", h, w_i, _epilogue_q, [OFF_Q], ATTN_WIDTH - Q_TAIL, TM, TN, lane_tiles=(cos_t, sin_t),
                m_chunks=Q_CHUNKS)
    act_gate = _matmul("in_proj_silu", h, w_i, functools.partial(_epilogue_act, act=_silu),
                       [OFF_GATE_A, OFF_GATE_B], ATTN_WIDTH, TM, TN)
    act_u = _matmul("in_proj_gelu_u", h, w_i, _epilogue_gelu_gated, [OFF_U], GMLP_WIDTH, TM, TN,
                    tiles=[(act_gate, nw)])
    act_v = _matmul("in_proj_gelu_v", h, w_i, functools.partial(_epilogue_act, act=_gelu),
                    [OFF_U + GMLP_WIDTH], GMLP_WIDTH, TM, TN)
    act_merge = _matmul("in_proj_sigmoid", h, w_i, functools.partial(_epilogue_act, act=_sigmoid),
                        [OFF_MERGE], 2 * D_MODEL, TM, TN)
    attn_g = _attention(q, qkv, act_gate, attn_sink[0])
    ya, w_out_b = _matmul("up_attn", attn_g, w_up_attn[0], _epilogue_up_attn, [0], D_MODEL, TM, TN,
                          tiles=[(act_merge, 0)], side_convert=w_out[0])
    sgb = _spatial_gating(act_u, act_v, gmlp_ln_g, gmlp_ln_b, w_spatial[0], b_spatial[0].T)
    merged = _matmul("up_gmlp_merge", sgb, w_up_gmlp[0], _epilogue_up_gmlp, [0], D_MODEL, TM, TN,
                     tiles=[(act_merge, nw), (ya, 0)])
    out = _out_proj(merged, w_out_b, x2, final_norm_g[None, :])
    return out.reshape(b, s, d)
```
